```python
import math
import jax, jax.numpy as jnp
from jax import lax
import numpy as np

D_MODEL = 2048
BATCH = 2
SEQ = 16384
DEPTH = 2

GRID_W = 64
CTX_LEN = 256
N_EVEN = (DEPTH + 1) // 2
N_ODD = DEPTH // 2
CONV_W = 3 * D_MODEL // 4
CONV_K = 31
SSM_W = D_MODEL // 4
SSM_P = 16
SSM_G = SSM_W // SSM_P
SSM_N = 64
EVEN_IN = 2 * CONV_W + SSM_W
EVEN_MIX = CONV_W + SSM_W
SGU_W = D_MODEL
SGU_CHUNK = 128
SGU_HEADS = 8
SGU_HD = SGU_W // SGU_HEADS
D_FF = 2 * D_MODEL
RMS_EPS = 1e-6
LN_EPS = 1e-5

kernel_name = 'hybrid_conv_s5_sgu_dit_block'


def rms_norm(x, g):
    x32 = x.astype(jnp.float32)
    y = x32 * lax.rsqrt(jnp.mean(x32 * x32, axis=-1, keepdims=True) + RMS_EPS)
    return (y * g.astype(jnp.float32)).astype(x.dtype)


def layer_norm(x, g, b):
    x32 = x.astype(jnp.float32)
    mu = jnp.mean(x32, axis=-1, keepdims=True)
    xc = x32 - mu
    var = jnp.mean(xc * xc, axis=-1, keepdims=True)
    return (xc * lax.rsqrt(var + LN_EPS) * g.astype(jnp.float32) + b.astype(jnp.float32)).astype(x.dtype)


def modulate(h, shift, scale):
    return h * (1.0 + scale[:, None, :]) + shift[:, None, :]


def dwconv1d(x, w, b):
    k = w.shape[0]
    y = lax.conv_general_dilated(x, w[:, None, :].astype(x.dtype), window_strides=(1,),
                                 padding=[(k // 2, k // 2)],
                                 dimension_numbers=('NWC', 'WIO', 'NWC'),
                                 feature_group_count=x.shape[-1])
    return y + b.astype(x.dtype)


def dwconv_grid(x, w, b):
    bn, length, ch = x.shape
    rows = length // GRID_W
    xg = x.reshape(bn, rows, GRID_W, ch)
    y = lax.conv_general_dilated(xg, w[:, :, None, :].astype(x.dtype), window_strides=(1, 1),
                                 padding=[(1, 1), (1, 1)],
                                 dimension_numbers=('NHWC', 'HWIO', 'NHWC'),
                                 feature_group_count=ch)
    return y.reshape(bn, length, ch) + b.astype(x.dtype)


def conformer_conv(a_val, a_gate, conv_w, conv_b, ln_g, ln_b):
    z = a_val * jax.nn.sigmoid(a_gate)
    z = dwconv1d(z, conv_w, conv_b)
    z = layer_norm(z, ln_g, ln_b)
    return jax.nn.silu(z)


def s5_discretize(a_re, a_im, log_step, b_re, b_im):
    a_re = a_re.astype(jnp.float32)
    a_im = a_im.astype(jnp.float32)
    b_re = b_re.astype(jnp.float32)
    b_im = b_im.astype(jnp.float32)
    dt = jnp.exp(log_step.astype(jnp.float32))[:, None]
    mag = jnp.exp(a_re * dt)
    ang = a_im * dt
    ab_re = mag * jnp.cos(ang)
    ab_im = mag * jnp.sin(ang)
    den = a_re * a_re + a_im * a_im
    num_re = ab_re - 1.0
    f_re = (num_re * a_re + ab_im * a_im) / den
    f_im = (ab_im * a_re - num_re * a_im) / den
    bb_re = f_re[..., None] * b_re - f_im[..., None] * b_im
    bb_im = f_re[..., None] * b_im + f_im[..., None] * b_re
    return ab_re, ab_im, bb_re, bb_im


def _linear_recurrence_combine(e_i, e_j):
    ai_re, ai_im, bi_re, bi_im = e_i
    aj_re, aj_im, bj_re, bj_im = e_j
    return (aj_re * ai_re - aj_im * ai_im,
            aj_re * ai_im + aj_im * ai_re,
            aj_re * bi_re - aj_im * bi_im + bj_re,
            aj_re * bi_im + aj_im * bi_re + bj_im)


def s5_states(u, a_re, a_im, log_step, b_re, b_im, h0, reverse):
    ab_re, ab_im, bb_re, bb_im = s5_discretize(a_re, a_im, log_step, b_re, b_im)
    bu_re = jnp.einsum('blgp,gnp->lbgn', u, bb_re)
    bu_im = jnp.einsum('blgp,gnp->lbgn', u, bb_im)
    length = u.shape[1]
    if h0 is not None:
        idx = length - 1 if reverse else 0
        h0_re, h0_im = h0
        bu_re = bu_re.at[idx].add(ab_re * h0_re - ab_im * h0_im)
        bu_im = bu_im.at[idx].add(ab_re * h0_im + ab_im * h0_re)
    a_re_l = jnp.broadcast_to(ab_re, (length, 1) + ab_re.shape)
    a_im_l = jnp.broadcast_to(ab_im, (length, 1) + ab_im.shape)
    _, _, h_re, h_im = lax.associative_scan(_linear_recurrence_combine,
                                            (a_re_l, a_im_l, bu_re, bu_im),
                                            reverse=reverse, axis=0)
    return h_re, h_im


def s5_bidir_states(u, a_re, a_im, log_step, b_re, b_im, h0s):
    states = []
    for d in range(2):
        h0 = None if h0s is None else h0s[d]
        states.append(s5_states(u, a_re[d], a_im[d], log_step[d], b_re[d], b_im[d], h0, reverse=(d == 1)))
    return states


def s5_finals(states):
    (f_re, f_im), (b_re, b_im) = states
    return [(f_re[-1], f_im[-1]), (b_re[0], b_im[0])]


def s5_readout(states, u, c_re, c_im, d_skip, glu_w, glu_b):
    bn, length = u.shape[:2]
    y = d_skip.astype(jnp.float32) * u.reshape(bn, length, SSM_W)
    for d, (h_re, h_im) in enumerate(states):
        y_d = (jnp.einsum('lbgn,gpn->blgp', h_re, c_re[d].astype(jnp.float32))
               - jnp.einsum('lbgn,gpn->blgp', h_im, c_im[d].astype(jnp.float32)))
        y = y + y_d.reshape(bn, length, SSM_W)
    y = jax.nn.gelu(y)
    return y * jax.nn.sigmoid(y @ glu_w.astype(jnp.float32) + glu_b.astype(jnp.float32))


def even_mixer(hn, w_in, conv_w, conv_b, ln_g, ln_b, ssm_in, ssm_out, w_out, h0s, want_finals):
    bn, length, _ = hn.shape
    proj = hn @ w_in
    a_val, a_gate, u = jnp.split(proj, [CONV_W, 2 * CONV_W], axis=-1)
    y_conv = conformer_conv(a_val, a_gate, conv_w, conv_b, ln_g, ln_b)
    u32 = u.astype(jnp.float32).reshape(bn, length, SSM_G, SSM_P)
    states = s5_bidir_states(u32, *ssm_in, h0s)
    y_ssm = s5_readout(states, u32, *ssm_out).astype(hn.dtype)
    out = jnp.concatenate([y_conv, y_ssm], axis=-1) @ w_out
    finals = s5_finals(states) if want_finals else None
    return out, finals


def chunk_mlp_mixer(hn, w_in, ln_g, ln_b, sgu_w, sgu_b, w_out):
    bn, length, _ = hn.shape
    z = jax.nn.gelu(hn @ w_in)
    u, v = jnp.split(z, 2, axis=-1)
    v = layer_norm(v, ln_g, ln_b)
    v = v.reshape(bn, length // SGU_CHUNK, SGU_CHUNK, SGU_HEADS, SGU_HD)
    s = jnp.einsum('hqk,bnkhc->bnqhc', sgu_w.astype(v.dtype), v)
    s = s + jnp.transpose(sgu_b).astype(v.dtype)[None, None, :, :, None]
    return (u * s.reshape(bn, length, SGU_W)) @ w_out


def conv_ffn(hn, w_in, conv_w, conv_b, w_out, on_grid):
    z = hn @ w_in
    z = dwconv_grid(z, conv_w, conv_b) if on_grid else dwconv1d(z, conv_w[1], conv_b)
    a, g = jnp.split(z, 2, axis=-1)
    return (a * jax.nn.silu(g)) @ w_out


def setup_inputs(seed: int = 0) -> dict:
    key = jax.random.key(seed)
    ks = iter(jax.random.split(key, 48))

    def nrm(shape, scale):
        return jax.random.normal(next(ks), shape, jnp.float32) * scale

    d = D_MODEL
    inputs = {}
    inputs['x'] = nrm((BATCH, SEQ, d), 1.0)
    inputs['c'] = nrm((BATCH, d), 1.0)
    inputs['ctx'] = nrm((BATCH, CTX_LEN, d), 1.0)
    inputs['c_ctx'] = nrm((d,), 1.0)
    inputs['ada_w'] = nrm((DEPTH, d, 6 * d), 0.5 * d ** -0.5)
    inputs['ada_b'] = nrm((DEPTH, 6 * d), 0.02)
    inputs['norm_g'] = 1.0 + nrm((DEPTH, 2, d), 0.02)
    inputs['final_g'] = 1.0 + nrm((d,), 0.02)
    inputs['e_w_in'] = nrm((N_EVEN, d, EVEN_IN), d ** -0.5)
    inputs['e_conv_w'] = nrm((N_EVEN, CONV_K, CONV_W), CONV_K ** -0.5)
    inputs['e_conv_b'] = nrm((N_EVEN, CONV_W), 0.02)
    inputs['e_ln_g'] = 1.0 + nrm((N_EVEN, CONV_W), 0.02)
    inputs['e_ln_b'] = nrm((N_EVEN, CONV_W), 0.02)
    inputs['s5_a_re'] = -0.5 + nrm((N_EVEN, 2, SSM_G, SSM_N), 0.01)
    inputs['s5_a_im'] = jnp.broadcast_to(jnp.pi * jnp.arange(SSM_N, dtype=jnp.float32),
                                         (N_EVEN, 2, SSM_G, SSM_N))
    inputs['s5_log_step'] = jax.random.uniform(next(ks), (N_EVEN, 2, SSM_G), jnp.float32,
                                               minval=math.log(1e-3), maxval=math.log(1e-1))
    inputs['s5_b_re'] = nrm((N_EVEN, 2, SSM_G, SSM_N, SSM_P), (2 * SSM_P) ** -0.5)
    inputs['s5_b_im'] = nrm((N_EVEN, 2, SSM_G, SSM_N, SSM_P), (2 * SSM_P) ** -0.5)
    inputs['s5_c_re'] = nrm((N_EVEN, 2, SSM_G, SSM_P, SSM_N), SSM_N ** -0.5)
    inputs['s5_c_im'] = nrm((N_EVEN, 2, SSM_G, SSM_P, SSM_N), SSM_N ** -0.5)
    inputs['s5_d'] = nrm((N_EVEN, SSM_W), 1.0)
    inputs['s5_glu_w'] = nrm((N_EVEN, SSM_W, SSM_W), SSM_W ** -0.5)
    inputs['s5_glu_b'] = nrm((N_EVEN, SSM_W), 0.02)
    inputs['e_w_out'] = nrm((N_EVEN, EVEN_MIX, d), EVEN_MIX ** -0.5)
    inputs['o_w_in'] = nrm((N_ODD, d, 2 * SGU_W), d ** -0.5)
    inputs['o_ln_g'] = 1.0 + nrm((N_ODD, SGU_W), 0.02)
    inputs['o_ln_b'] = nrm((N_ODD, SGU_W), 0.02)
    inputs['o_sgu_w'] = nrm((N_ODD, SGU_HEADS, SGU_CHUNK, SGU_CHUNK), SGU_CHUNK ** -0.5)
    inputs['o_sgu_b'] = 1.0 + nrm((N_ODD, SGU_HEADS, SGU_CHUNK), 0.1)
    inputs['o_w_out'] = nrm((N_ODD, SGU_W, d), SGU_W ** -0.5)
    inputs['f_w_in'] = nrm((DEPTH, d, 2 * D_FF), d ** -0.5)
    inputs['f_conv_w'] = nrm((DEPTH, 3, 3, 2 * D_FF), 1.0 / 3.0)
    inputs['f_conv_b'] = nrm((DEPTH, 2 * D_FF), 0.02)
    inputs['f_w_out'] = nrm((DEPTH, D_FF, d), D_FF ** -0.5)
    return inputs


def reference(x, c, ctx, c_ctx, ada_w, ada_b, norm_g, final_g,
              e_w_in, e_conv_w, e_conv_b, e_ln_g, e_ln_b,
              s5_a_re, s5_a_im, s5_log_step, s5_b_re, s5_b_im, s5_c_re, s5_c_im,
              s5_d, s5_glu_w, s5_glu_b, e_w_out,
              o_w_in, o_ln_g, o_ln_b, o_sgu_w, o_sgu_b, o_w_out,
              f_w_in, f_conv_w, f_conv_b, f_w_out):
    d = D_MODEL
    h_lat = x
    h_ctx = ctx
    silu_c = jax.nn.silu(c)
    silu_cc = jax.nn.silu(c_ctx)[None, :]
    for layer in range(DEPTH):
        is_even = layer % 2 == 0
        li = layer // 2
        ctx_out_needed = any(j % 2 == 0 for j in range(layer + 1, DEPTH))
        n_ctx_mod = 6 if ctx_out_needed else (2 if is_even else 0)

        sh1, sc1, g1, sh2, sc2, g2 = jnp.split(silu_c @ ada_w[layer] + ada_b[layer], 6, axis=-1)
        if n_ctx_mod:
            cmod = jnp.split(silu_cc @ ada_w[layer][:, :n_ctx_mod * d] + ada_b[layer][:n_ctx_mod * d],
                             n_ctx_mod, axis=-1)
            cn = modulate(rms_norm(h_ctx, norm_g[layer, 0]), cmod[0], cmod[1])
        hn = modulate(rms_norm(h_lat, norm_g[layer, 0]), sh1, sc1)

        if is_even:
            ssm_in = (s5_a_re[li], s5_a_im[li], s5_log_step[li], s5_b_re[li], s5_b_im[li])
            ssm_out = (s5_c_re[li], s5_c_im[li], s5_d[li], s5_glu_w[li], s5_glu_b[li])
            if ctx_out_needed:
                c_mix, h0s = even_mixer(cn, e_w_in[li], e_conv_w[li], e_conv_b[li], e_ln_g[li], e_ln_b[li],
                                        ssm_in, ssm_out, e_w_out[li], None, True)
            else:
                u_c = (cn @ e_w_in[li][:, 2 * CONV_W:]).astype(jnp.float32)
                u_c = u_c.reshape(u_c.shape[0], u_c.shape[1], SSM_G, SSM_P)
                h0s = s5_finals(s5_bidir_states(u_c, *ssm_in, None))
            mix, _ = even_mixer(hn, e_w_in[li], e_conv_w[li], e_conv_b[li], e_ln_g[li], e_ln_b[li],
                                ssm_in, ssm_out, e_w_out[li], h0s, False)
        else:
            mix = chunk_mlp_mixer(hn, o_w_in[li], o_ln_g[li], o_ln_b[li], o_sgu_w[li], o_sgu_b[li], o_w_out[li])
            if ctx_out_needed:
                c_mix = chunk_mlp_mixer(cn, o_w_in[li], o_ln_g[li], o_ln_b[li], o_sgu_w[li], o_sgu_b[li],
                                        o_w_out[li])

        h_lat = h_lat + g1[:, None, :] * mix
        hn2 = modulate(rms_norm(h_lat, norm_g[layer, 1]), sh2, sc2)
        h_lat = h_lat + g2[:, None, :] * conv_ffn(hn2, f_w_in[layer], f_conv_w[layer], f_conv_b[layer],
                                                  f_w_out[layer], True)
        if ctx_out_needed:
            h_ctx = h_ctx + cmod[2][:, None, :] * c_mix
            cn2 = modulate(rms_norm(h_ctx, norm_g[layer, 1]), cmod[3], cmod[4])
            h_ctx = h_ctx + cmod[5][:, None, :] * conv_ffn(cn2, f_w_in[layer], f_conv_w[layer], f_conv_b[layer],
                                                           f_w_out[layer], False)
    return rms_norm(h_lat, final_g)
```

```python
import functools
import math

import jax
import jax.numpy as jnp
from jax import lax
from jax.experimental import pallas as pl
from jax.experimental.pallas import tpu as pltpu

F32 = jnp.float32
BF16 = jnp.bfloat16
HIGHEST = lax.Precision.HIGHEST

RMS_EPS = 1e-6
LN_EPS = 1e-5
GRID_W = 64
CONV_K = 31
CONV_HALO = 16
SSM_P = 16
SSM_N = 64
SGU_CHUNK = 128
SGU_HEADS = 8
T_CHUNK = 128
LANES = 128
SUBLANES = 8
VMEM_LIMIT_BYTES = 56 * 1024 * 1024


def _cparams(ngrid):
    return pltpu.CompilerParams(dimension_semantics=("arbitrary",) * ngrid,
                                vmem_limit_bytes=VMEM_LIMIT_BYTES)


def _resident(shape):
    nd = len(shape)
    return pl.BlockSpec(shape, lambda *_: (0,) * nd, pipeline_mode=pl.Buffered(1))


def _sigmoid(x):
    return 1.0 / (1.0 + jnp.exp(-x))


def _silu(x):
    return x * _sigmoid(x)


def _gelu(x):
    return 0.5 * x * (1.0 + jnp.tanh(math.sqrt(2.0 / math.pi) * (x + 0.044715 * (x * x * x))))


def _norm_mod(x, g, shift, scale):
    ms = jnp.mean(x * x, axis=-1, keepdims=True)
    y = x * lax.rsqrt(ms + RMS_EPS) * g
    return y * (1.0 + scale) + shift


def _layer_norm(x, g, b):
    mu = jnp.mean(x, axis=-1, keepdims=True)
    xc = x - mu
    var = jnp.mean(xc * xc, axis=-1, keepdims=True)
    return xc * lax.rsqrt(var + LN_EPS) * g + b


def _ada_body(s_ref, w_ref, b_ref, o_ref):
    s = _silu(s_ref[...])
    o_ref[0] = jnp.dot(s, w_ref[0], preferred_element_type=F32, precision=HIGHEST) + b_ref[0]


def _ada_call(s_rows, ada_w, ada_b):
    depth, d, n = ada_w.shape
    tn = 1024
    return pl.pallas_call(
        _ada_body,
        grid=(depth, n // tn),
        in_specs=[pl.BlockSpec((SUBLANES, d), lambda l, j: (0, 0)),
                  pl.BlockSpec((1, d, tn), lambda l, j: (l, 0, j)),
                  pl.BlockSpec((1, 1, tn), lambda l, j: (l, 0, j))],
        out_specs=pl.BlockSpec((1, SUBLANES, tn), lambda l, j: (l, 0, j)),
        out_shape=jax.ShapeDtypeStruct((depth, SUBLANES, n), F32),
        compiler_params=_cparams(2),
        name="ada_mod",
    )(s_rows, ada_w, ada_b.reshape(depth, 1, n))


def _even_in_body(x_ref, g_ref, sh_ref, sc_ref, wv_ref, wg_ref, wut_ref, z_ref, ut_ref):
    hn = _norm_mod(x_ref[...], g_ref[...], sh_ref[0], sc_ref[0]).astype(BF16)
    av = jnp.dot(hn, wv_ref[...], preferred_element_type=F32)
    ag = jnp.dot(hn, wg_ref[...], preferred_element_type=F32)
    z_ref[...] = av * _sigmoid(ag)
    ut_ref[...] = lax.dot_general(wut_ref[...], hn, (((1,), (1,)), ((), ())),
                                  preferred_element_type=F32)


def _even_in_call(x2d, g, shift, scale, wv, wg, wut, seq_len, tm):
    tokens, d = x2d.shape
    cw = wv.shape[1]
    sw = wut.shape[0]
    tiles_per_seq = seq_len // tm
    mod_spec = pl.BlockSpec((1, 1, d), lambda i: (i // tiles_per_seq, 0, 0))
    return pl.pallas_call(
        _even_in_body,
        grid=(tokens // tm,),
        in_specs=[pl.BlockSpec((tm, d), lambda i: (i, 0)),
                  _resident((1, d)), mod_spec, mod_spec,
                  _resident((d, cw)), _resident((d, cw)), _resident((sw, d))],
        out_specs=[pl.BlockSpec((tm, cw), lambda i: (i, 0)),
                   pl.BlockSpec((sw, tm), lambda i: (0, i))],
        out_shape=[jax.ShapeDtypeStruct((tokens, cw), F32),
                   jax.ShapeDtypeStruct((sw, tokens), F32)],
        compiler_params=_cparams(1),
        name="even_in_proj",
    )(x2d, g, shift, scale, wv, wg, wut)


def _ctx_in_body(x_ref, g_ref, sh_ref, sc_ref, wut_ref, ut_ref):
    hn = _norm_mod(x_ref[...], g_ref[...], sh_ref[...], sc_ref[...]).astype(BF16)
    ut_ref[...] = lax.dot_general(wut_ref[...], hn, (((1,), (1,)), ((), ())),
                                  preferred_element_type=F32)


def _ctx_in_call(c2d, g, shift, scale, wut):
    tokens, d = c2d.shape
    sw = wut.shape[0]
    return pl.pallas_call(
        _ctx_in_body,
        out_shape=jax.ShapeDtypeStruct((sw, tokens), F32),
        compiler_params=pltpu.CompilerParams(vmem_limit_bytes=VMEM_LIMIT_BYTES),
        name="ctx_in_proj",
    )(c2d, g, shift, scale, wut)


CONV_ROWS = 64


def _conv_body(zp_ref, zm_ref, zn_ref, w_ref, b_ref, g_ref, beta_ref, o_ref, buf_ref, acc_ref,
               *, tl, tiles_per_seq):
    i = pl.program_id(0)
    first = (i % tiles_per_seq) == 0
    last = (i % tiles_per_seq) == tiles_per_seq - 1
    buf_ref[0:CONV_HALO, :] = jnp.where(first, 0.0, zp_ref[...])
    buf_ref[CONV_HALO:CONV_HALO + tl, :] = zm_ref[...]
    buf_ref[CONV_HALO + tl:2 * CONV_HALO + tl, :] = jnp.where(last, 0.0, zn_ref[...])

    cw = o_ref.shape[1]
    n_rb = tl // CONV_ROWS
    n_cb = cw // LANES
    rows = CONV_ROWS + SUBLANES

    def block(idx, carry):
        rb = idx // n_cb
        cb = idx % n_cb
        r0 = pl.multiple_of(rb * CONV_ROWS, CONV_ROWS)
        c0 = pl.multiple_of(cb * LANES, LANES)
        acc = jnp.zeros((CONV_ROWS, LANES), F32)
        for r in range(SUBLANES):
            part = None
            for a in range(4):
                k = SUBLANES * a + r - 1
                if k < 0 or k >= CONV_K:
                    continue
                term = buf_ref[pl.ds(r0 + SUBLANES * a, rows), pl.ds(c0, LANES)] * \
                    w_ref[pl.ds(k, 1), pl.ds(c0, LANES)]
                part = term if part is None else part + term
            if r:
                part = pltpu.roll(part, rows - r, axis=0)
            acc = acc + part[:CONV_ROWS]
        acc_ref[pl.ds(r0, CONV_ROWS), pl.ds(c0, LANES)] = acc + b_ref[:, pl.ds(c0, LANES)]
        return carry

    lax.fori_loop(0, n_rb * n_cb, block, 0)

    ln_rows = 32

    def ln_block(rb, carry):
        r0 = pl.multiple_of(rb * ln_rows, ln_rows)
        y = _layer_norm(acc_ref[pl.ds(r0, ln_rows), :], g_ref[...], beta_ref[...])
        o_ref[pl.ds(r0, ln_rows), :] = _silu(y).astype(o_ref.dtype)
        return carry

    lax.fori_loop(0, tl // ln_rows, ln_block, 0)


def _conv_call(z2d, conv_w, conv_b, ln_g, ln_b, seq_len, tl):
    tokens, cw = z2d.shape
    tiles_per_seq = seq_len // tl
    hb = tl // CONV_HALO
    n_hblocks = tokens // CONV_HALO
    w_pad = jnp.pad(conv_w, ((0, 32 - CONV_K), (0, 0)))
    body = functools.partial(_conv_body, tl=tl, tiles_per_seq=tiles_per_seq)
    return pl.pallas_call(
        body,
        grid=(tokens // tl,),
        in_specs=[pl.BlockSpec((CONV_HALO, cw), lambda i: (jnp.maximum(i * hb - 1, 0), 0)),
                  pl.BlockSpec((tl, cw), lambda i: (i, 0)),
                  pl.BlockSpec((CONV_HALO, cw), lambda i: (jnp.minimum((i + 1) * hb, n_hblocks - 1), 0)),
                  _resident((32, cw)), _resident((1, cw)), _resident((1, cw)), _resident((1, cw))],
        out_specs=pl.BlockSpec((tl, cw), lambda i: (i, 0)),
        out_shape=jax.ShapeDtypeStruct((tokens, cw), BF16),
        scratch_shapes=[pltpu.VMEM((tl + 2 * CONV_HALO, cw), F32), pltpu.VMEM((tl, cw), F32)],
        compiler_params=_cparams(1),
        name="conformer_conv",
    )(z2d, z2d, z2d, w_pad, conv_b.reshape(1, cw), ln_g.reshape(1, cw), ln_b.reshape(1, cw))


def _s5_prep_body(ar_row_ref, ai_row_ref, ar_col_ref, ai_col_ref, ls_ref,
                  bt_re_ref, bt_im_ref, c_re_ref, c_im_ref, ct_re_ref, ct_im_ref, dpq_ref,
                  toep_ref, bs_ref, wst_ref, coef_ref, kv_ref):
    t = T_CHUNK
    n = SSM_N
    p_dim = SSM_P
    lane2 = lax.broadcasted_iota(jnp.int32, (1, 2 * t), 1)
    lane1 = lax.broadcasted_iota(jnp.int32, (1, t), 1).astype(F32)
    sub1 = lax.broadcasted_iota(jnp.int32, (t, 1), 0).astype(F32)

    bs_ref[...] = jnp.zeros(bs_ref.shape, bs_ref.dtype)
    wst_ref[...] = jnp.zeros(wst_ref.shape, wst_ref.dtype)
    coef_ref[...] = jnp.zeros(coef_ref.shape, coef_ref.dtype)

    kv = dpq_ref[0] * (lane2 == t).astype(F32)
    for d in range(2):
        dt = jnp.exp(ls_ref[0, d])
        ar = ar_row_ref[0, d]
        ai = ai_row_ref[0, d]
        lam_row = ar * dt
        ang_row = ai * dt
        lam_col = ar_col_ref[0, d] * dt
        ang_col = ai_col_ref[0, d] * dt
        mag = jnp.exp(lam_row)
        ab_re = mag * jnp.cos(ang_row)
        ab_im = mag * jnp.sin(ang_row)
        den = ar * ar + ai * ai
        num_re = ab_re - 1.0
        f_re = (num_re * ar + ab_im * ai) / den
        f_im = (ab_im * ar - num_re * ai) / den
        bbt_re = f_re * bt_re_ref[0, d] - f_im * bt_im_ref[0, d]
        bbt_im = f_re * bt_im_ref[0, d] + f_im * bt_re_ref[0, d]

        if d == 0:
            lag2 = jnp.maximum(lane2 - t, 0).astype(F32)
            msk2 = (lane2 >= t).astype(F32)
        else:
            lag2 = jnp.maximum(t - lane2, 0).astype(F32)
            msk2 = (lane2 <= t).astype(F32)
        pm = jnp.exp(lam_col * lag2) * msk2
        pt_re = pm * jnp.cos(ang_col * lag2)
        pt_im = pm * jnp.sin(ang_col * lag2)
        cb_re = jnp.concatenate(
            [c_re_ref[0, d, pl.ds(p, 1), :] * bbt_re - c_im_ref[0, d, pl.ds(p, 1), :] * bbt_im
             for p in range(p_dim)], axis=0)
        cb_im = jnp.concatenate(
            [c_re_ref[0, d, pl.ds(p, 1), :] * bbt_im + c_im_ref[0, d, pl.ds(p, 1), :] * bbt_re
             for p in range(p_dim)], axis=0)
        kv = kv + (jnp.dot(cb_re, pt_re, preferred_element_type=F32, precision=HIGHEST)
                   - jnp.dot(cb_im, pt_im, preferred_element_type=F32, precision=HIGHEST))

        lag_col = (t - 1.0 - sub1) if d == 0 else sub1
        pw_m = jnp.exp(lag_col * lam_row)
        pw_re = pw_m * jnp.cos(lag_col * ang_row)
        pw_im = pw_m * jnp.sin(lag_col * ang_row)
        for q in range(p_dim):
            br = bbt_re[q:q + 1, :]
            bi = bbt_im[q:q + 1, :]
            bs_ref[0, q * t:(q + 1) * t, (2 * d) * LANES:(2 * d) * LANES + n] = \
                (pw_re * br - pw_im * bi).astype(bs_ref.dtype)
            bs_ref[0, q * t:(q + 1) * t, (2 * d + 1) * LANES:(2 * d + 1) * LANES + n] = \
                (pw_re * bi + pw_im * br).astype(bs_ref.dtype)

        lag_row = (lane1 + 1.0) if d == 0 else (t - lane1)
        qm = jnp.exp(lam_col * lag_row)
        q_re = qm * jnp.cos(ang_col * lag_row)
        q_im = qm * jnp.sin(ang_col * lag_row)
        for p in range(p_dim):
            cr = ct_re_ref[0, d, :, p:p + 1]
            ci = ct_im_ref[0, d, :, p:p + 1]
            wst_ref[0, (2 * d) * LANES:(2 * d) * LANES + n, p * t:(p + 1) * t] = \
                (cr * q_re - ci * q_im).astype(wst_ref.dtype)
            wst_ref[0, (2 * d + 1) * LANES:(2 * d + 1) * LANES + n, p * t:(p + 1) * t] = \
                (-(cr * q_im + ci * q_re)).astype(wst_ref.dtype)

        tt = float(t)
        at_m = jnp.exp(lam_row * tt)
        coef_ref[0, 2 * d:2 * d + 1, 0:n] = at_m * jnp.cos(ang_row * tt)
        coef_ref[0, 2 * d + 1:2 * d + 2, 0:n] = at_m * jnp.sin(ang_row * tt)

    kv_ref[...] = kv

    def toep_block(idx, carry):
        p = idx // p_dim
        q = idx % p_dim
        row = jnp.broadcast_to(kv_ref[pl.ds(idx, 1), :], (t, 2 * t))
        rolled = pltpu.roll(row, 0, axis=1, stride=1, stride_axis=0)
        toep_ref[0, pl.ds(pl.multiple_of(q * t, t), t), pl.ds(pl.multiple_of(p * t, t), t)] = \
            rolled[:, t:].astype(toep_ref.dtype)
        return carry

    lax.fori_loop(0, p_dim * p_dim, toep_block, 0)


def _s5_prep_call(a_re, a_im, log_step, b_re, b_im, c_re, c_im, d_skip):
    g_dim = a_re.shape[1]
    n, p_dim, t = SSM_N, SSM_P, T_CHUNK
    gm = lambda v: jnp.swapaxes(v, 0, 1)
    ar, ai = gm(a_re), gm(a_im)
    dpq = (jnp.eye(p_dim, dtype=F32)[None] * d_skip.reshape(g_dim, p_dim, 1)).reshape(g_dim, p_dim * p_dim, 1)
    args = (ar[:, :, None, :], ai[:, :, None, :], ar[:, :, :, None], ai[:, :, :, None],
            gm(log_step)[:, :, None, None],
            jnp.swapaxes(gm(b_re), 2, 3), jnp.swapaxes(gm(b_im), 2, 3),
            gm(c_re), gm(c_im), jnp.swapaxes(gm(c_re), 2, 3), jnp.swapaxes(gm(c_im), 2, 3), dpq)

    def spec(v):
        blk = (1,) + v.shape[1:]
        nd = v.ndim
        return pl.BlockSpec(blk, lambda g: (g,) + (0,) * (nd - 1))

    out_shapes = [jax.ShapeDtypeStruct((g_dim, p_dim * t, p_dim * t), BF16),
                  jax.ShapeDtypeStruct((g_dim, p_dim * t, 4 * LANES), BF16),
                  jax.ShapeDtypeStruct((g_dim, 4 * LANES, p_dim * t), BF16),
                  jax.ShapeDtypeStruct((g_dim, SUBLANES, LANES), F32)]
    return pl.pallas_call(
        _s5_prep_body,
        grid=(g_dim,),
        in_specs=[spec(v) for v in args],
        out_specs=[spec(s) for s in out_shapes],
        out_shape=out_shapes,
        scratch_shapes=[pltpu.VMEM((p_dim * p_dim, 2 * t), F32)],
        compiler_params=_cparams(1),
        name="s5_prep",
    )(*args)


def _s5_body(*refs, nb, nc, with_y):
    if with_y:
        u_ref, bs_ref, coef_ref, h0_ref, toep_ref, wst_ref, y_ref, fin_ref, s_scr, h_scr = refs
    else:
        u_ref, bs_ref, coef_ref, h0_ref, fin_ref, s_scr = refs
    p_dim = u_ref.shape[1]
    t = T_CHUNK
    u = jnp.concatenate([u_ref[0, q] for q in range(p_dim)], axis=1).astype(BF16)
    s_scr[...] = jnp.dot(u, bs_ref[0], preferred_element_type=F32)
    if with_y:
        h_scr[...] = jnp.zeros(h_scr.shape, h_scr.dtype)

    af_re, af_im = coef_ref[0, 0:1, :], coef_ref[0, 1:2, :]
    ab_re, ab_im = coef_ref[0, 2:3, :], coef_ref[0, 3:4, :]
    init = tuple(h0_ref[0, r:r + 1, :] for r in range(4 * nb))

    blk = SUBLANES if nc % SUBLANES == 0 else nc

    def block(k, carry):
        new = list(carry)
        for b in range(nb):
            f0 = b * nc + blk * k
            b0 = b * nc + nc - blk - blk * k
            if blk == SUBLANES:
                f0 = pl.multiple_of(f0, SUBLANES)
                b0 = pl.multiple_of(b0, SUBLANES)
            sf_re = s_scr[pl.ds(f0, blk), 0:LANES]
            sf_im = s_scr[pl.ds(f0, blk), LANES:2 * LANES]
            sb_re = s_scr[pl.ds(b0, blk), 2 * LANES:3 * LANES]
            sb_im = s_scr[pl.ds(b0, blk), 3 * LANES:4 * LANES]
            fr, fi, br, bi = new[4 * b:4 * b + 4]
            hf_re, hf_im, hb_re, hb_im = [], [], [], []
            for j in range(blk):
                jb = blk - 1 - j
                hf_re.append(fr)
                hf_im.append(fi)
                hb_re.insert(0, br)
                hb_im.insert(0, bi)
                fr, fi = (af_re * fr - af_im * fi + sf_re[j:j + 1],
                          af_re * fi + af_im * fr + sf_im[j:j + 1])
                br, bi = (ab_re * br - ab_im * bi + sb_re[jb:jb + 1],
                          ab_re * bi + ab_im * br + sb_im[jb:jb + 1])
            if with_y:
                h_scr[pl.ds(f0, blk), 0:LANES] = jnp.concatenate(hf_re, axis=0)
                h_scr[pl.ds(f0, blk), LANES:2 * LANES] = jnp.concatenate(hf_im, axis=0)
                h_scr[pl.ds(b0, blk), 2 * LANES:3 * LANES] = jnp.concatenate(hb_re, axis=0)
                h_scr[pl.ds(b0, blk), 3 * LANES:4 * LANES] = jnp.concatenate(hb_im, axis=0)
            new[4 * b:4 * b + 4] = [fr, fi, br, bi]
        return tuple(new)

    if nc == blk:
        fin = block(0, init)
    else:
        fin = lax.fori_loop(0, nc // blk, block, init)
    fin_ref[0] = jnp.concatenate(list(fin) + [jnp.zeros((1, LANES), F32)] * (SUBLANES - 4 * nb), axis=0) \
        if 4 * nb < SUBLANES else jnp.concatenate(list(fin), axis=0)

    if with_y:
        y = jnp.dot(u, toep_ref[0], preferred_element_type=F32)
        y = y + jnp.dot(h_scr[...].astype(BF16), wst_ref[0], preferred_element_type=F32)
        y = _gelu(y)
        for p in range(p_dim):
            y_ref[0, p] = y[:, p * t:(p + 1) * t]


def _s5_call(u4, bs, coef, h0, toep=None, wst=None, *, nb, nc):
    g_dim, p_dim, rows, t = u4.shape
    with_y = toep is not None
    body = functools.partial(_s5_body, nb=nb, nc=nc, with_y=with_y)

    def spec(v):
        blk = (1,) + v.shape[1:]
        nd = v.ndim
        return pl.BlockSpec(blk, lambda g: (g,) + (0,) * (nd - 1))

    fin_shape = jax.ShapeDtypeStruct((g_dim, SUBLANES, LANES), F32)
    args = [u4, bs, coef, h0]
    scratch = [pltpu.VMEM((rows, 4 * LANES), F32)]
    if with_y:
        args += [toep, wst]
        out_shape = [jax.ShapeDtypeStruct(u4.shape, F32), fin_shape]
        scratch.append(pltpu.VMEM((rows, 4 * LANES), F32))
    else:
        out_shape = [fin_shape]
    outs = pl.pallas_call(
        body,
        grid=(g_dim,),
        in_specs=[spec(v) for v in args],
        out_specs=[spec(s) for s in out_shape],
        out_shape=out_shape,
        scratch_shapes=scratch,
        compiler_params=_cparams(1),
        name="s5_chunks" if with_y else "s5_ctx_states",
    )(*args)
    return outs


def _even_out_body(yc_ref, yt_ref, gwt_ref, gb_ref, woa_ref, wob_ref, x_ref, g1_ref, o_ref):
    yt = yt_ref[...]
    gl = jnp.dot(gwt_ref[...], yt.astype(BF16), preferred_element_type=F32) + gb_ref[...]
    ys = (yt * _sigmoid(gl)).astype(BF16)
    mix = jnp.dot(yc_ref[...], woa_ref[...], preferred_element_type=F32)
    mix = mix + jnp.dot(ys.T, wob_ref[...], preferred_element_type=F32)
    o_ref[...] = x_ref[...] + g1_ref[0] * mix


def _even_out_call(yc, yt, gwt, gb, woa, wob, x2d, g1, seq_len, tm):
    tokens, d = x2d.shape
    cw = yc.shape[1]
    sw = yt.shape[0]
    tiles_per_seq = seq_len // tm
    return pl.pallas_call(
        _even_out_body,
        grid=(tokens // tm,),
        in_specs=[pl.BlockSpec((tm, cw), lambda i: (i, 0)),
                  pl.BlockSpec((sw, tm), lambda i: (0, i)),
                  _resident((sw, sw)), _resident((sw, 1)),
                  _resident((cw, d)), _resident((sw, d)),
                  pl.BlockSpec((tm, d), lambda i: (i, 0)),
                  pl.BlockSpec((1, 1, d), lambda i: (i // tiles_per_seq, 0, 0))],
        out_specs=pl.BlockSpec((tm, d), lambda i: (i, 0)),
        out_shape=jax.ShapeDtypeStruct((tokens, d), F32),
        compiler_params=_cparams(1),
        name="even_out_proj",
    )(yc, yt, gwt, gb, woa, wob, x2d, g1)


def _odd_body(x_ref, g_ref, sh_ref, sc_ref, wu_ref, wv_ref, lg_ref, lb_ref, sw_ref, sb_ref,
              wo_ref, g1_ref, o_ref, gated_ref):
    x = x_ref[...]
    tm = x.shape[0]
    hn = _norm_mod(x, g_ref[...], sh_ref[0], sc_ref[0]).astype(BF16)
    v = _gelu(jnp.dot(hn, wv_ref[...], preferred_element_type=F32))
    v = _layer_norm(v, lg_ref[...], lb_ref[...]).astype(BF16)
    u = _gelu(jnp.dot(hn, wu_ref[...], preferred_element_type=F32))
    hd = v.shape[1] // SGU_HEADS
    for ch in range(tm // SGU_CHUNK):
        r0 = ch * SGU_CHUNK
        for h in range(SGU_HEADS):
            s = jnp.dot(sw_ref[h], v[r0:r0 + SGU_CHUNK, h * hd:(h + 1) * hd],
                        preferred_element_type=F32) + sb_ref[h]
            gated_ref[r0:r0 + SGU_CHUNK, h * hd:(h + 1) * hd] = \
                (u[r0:r0 + SGU_CHUNK, h * hd:(h + 1) * hd] * s).astype(BF16)
    mix = jnp.dot(gated_ref[...], wo_ref[...], preferred_element_type=F32)
    o_ref[...] = x + g1_ref[0] * mix


def _odd_call(x2d, g, shift, scale, wu, wv, ln_g, ln_b, sgu_w, sgu_b, wo, g1, seq_len, tm):
    tokens, d = x2d.shape
    w = wu.shape[1]
    tiles_per_seq = seq_len // tm
    mod_spec = pl.BlockSpec((1, 1, d), lambda i: (i // tiles_per_seq, 0, 0))
    return pl.pallas_call(
        _odd_body,
        grid=(tokens // tm,),
        in_specs=[pl.BlockSpec((tm, d), lambda i: (i, 0)),
                  _resident((1, d)), mod_spec, mod_spec,
                  _resident((d, w)), _resident((d, w)), _resident((1, w)), _resident((1, w)),
                  _resident(sgu_w.shape), _resident(sgu_b.shape),
                  _resident((w, d)), mod_spec],
        out_specs=pl.BlockSpec((tm, d), lambda i: (i, 0)),
        out_shape=jax.ShapeDtypeStruct((tokens, d), F32),
        scratch_shapes=[pltpu.VMEM((tm, w), BF16)],
        compiler_params=_cparams(1),
        name="odd_gmlp",
    )(x2d, g, shift, scale, wu, wv, ln_g, ln_b, sgu_w, sgu_b, wo, g1)


def _ffn_up_body(x_ref, g_ref, sh_ref, sc_ref, w_ref, z_ref, hn_ref):
    @pl.when(pl.program_id(1) == 0)
    def _():
        hn_ref[...] = _norm_mod(x_ref[...], g_ref[...], sh_ref[0], sc_ref[0]).astype(BF16)

    z_ref[...] = jnp.dot(hn_ref[...], w_ref[...], preferred_element_type=F32).astype(z_ref.dtype)


def _ffn_up_call(x2d, g, shift, scale, w, seq_len, tm, tn):
    tokens, d = x2d.shape
    n = w.shape[1]
    tiles_per_seq = seq_len // tm
    mod_spec = pl.BlockSpec((1, 1, d), lambda i, j: (i // tiles_per_seq, 0, 0))
    return pl.pallas_call(
        _ffn_up_body,
        grid=(tokens // tm, n // tn),
        in_specs=[pl.BlockSpec((tm, d), lambda i, j: (i, 0)),
                  _resident((1, d)), mod_spec, mod_spec,
                  pl.BlockSpec((d, tn), lambda i, j: (0, j))],
        out_specs=pl.BlockSpec((tm, tn), lambda i, j: (i, j)),
        out_shape=jax.ShapeDtypeStruct((tokens, n), BF16),
        scratch_shapes=[pltpu.VMEM((tm, d), BF16)],
        compiler_params=_cparams(2),
        name="ffn_up_proj",
    )(x2d, g, shift, scale, w)


def _ffn_down_body(zp_ref, zm_ref, zn_ref, cw_ref, cb_ref, wo_ref, x_ref, g2_ref, fg_ref, o_ref,
                   act_ref, *, tm, tiles_per_seq, final_norm):
    i = pl.program_id(0)
    first = (i % tiles_per_seq) == 0
    last = (i % tiles_per_seq) == tiles_per_seq - 1
    dff = act_ref.shape[1]
    n_rb = tm // GRID_W
    n_cb = dff // LANES
    row_id = lax.broadcasted_iota(jnp.int32, (GRID_W, LANES), 0)
    has_left = row_id != 0
    has_right = row_id != GRID_W - 1
    top_ok = jnp.where(first, 0.0, 1.0)
    bot_ok = jnp.where(last, 0.0, 1.0)

    def conv_rows(rb, c0):
        r0 = pl.multiple_of(rb * GRID_W, GRID_W)
        mid = zm_ref[pl.ds(r0, GRID_W), pl.ds(c0, LANES)].astype(F32)
        r_up = pl.multiple_of(jnp.maximum(r0 - GRID_W, 0), GRID_W)
        r_dn = pl.multiple_of(jnp.minimum(r0 + GRID_W, tm - GRID_W), GRID_W)
        up_main = zm_ref[pl.ds(r_up, GRID_W), pl.ds(c0, LANES)].astype(F32)
        up_halo = zp_ref[:, pl.ds(c0, LANES)].astype(F32) * top_ok
        up = jnp.where(rb == 0, up_halo, up_main)
        dn_main = zm_ref[pl.ds(r_dn, GRID_W), pl.ds(c0, LANES)].astype(F32)
        dn_halo = zn_ref[:, pl.ds(c0, LANES)].astype(F32) * bot_ok
        dn = jnp.where(rb == n_rb - 1, dn_halo, dn_main)
        w = lambda k: cw_ref[pl.ds(k, 1), pl.ds(c0, LANES)]
        left = up * w(0) + mid * w(3) + dn * w(6)
        cent = up * w(1) + mid * w(4) + dn * w(7)
        right = up * w(2) + mid * w(5) + dn * w(8)
        left = jnp.where(has_left, pltpu.roll(left, 1, axis=0), 0.0)
        right = jnp.where(has_right, pltpu.roll(right, GRID_W - 1, axis=0), 0.0)
        return cent + left + right + cb_ref[:, pl.ds(c0, LANES)]

    def block(idx, carry):
        rb = idx // n_cb
        cb = idx % n_cb
        ca = pl.multiple_of(cb * LANES, LANES)
        cg = pl.multiple_of(dff + cb * LANES, LANES)
        a = conv_rows(rb, ca)
        g = conv_rows(rb, cg)
        act_ref[pl.ds(pl.multiple_of(rb * GRID_W, GRID_W), GRID_W), pl.ds(ca, LANES)] = \
            (a * _silu(g)).astype(BF16)
        return carry

    lax.fori_loop(0, n_rb * n_cb, block, 0)

    y = x_ref[...] + g2_ref[0] * jnp.dot(act_ref[...], wo_ref[...], preferred_element_type=F32)
    if final_norm:
        ms = jnp.mean(y * y, axis=-1, keepdims=True)
        y = y * lax.rsqrt(ms + RMS_EPS) * fg_ref[...]
    o_ref[...] = y


def _ffn_down_call(z, conv_w9, conv_b, wo, x2d, g2, final_g, seq_len, tm, final_norm):
    tokens, d = x2d.shape
    n2 = z.shape[1]
    dff = n2 // 2
    tiles_per_seq = seq_len // tm
    hb = tm // GRID_W
    n_hblocks = tokens // GRID_W
    cw_pad = jnp.pad(conv_w9, ((0, 16 - 9), (0, 0)))
    body = functools.partial(_ffn_down_body, tm=tm, tiles_per_seq=tiles_per_seq, final_norm=final_norm)
    return pl.pallas_call(
        body,
        grid=(tokens // tm,),
        in_specs=[pl.BlockSpec((GRID_W, n2), lambda i: (jnp.maximum(i * hb - 1, 0), 0)),
                  pl.BlockSpec((tm, n2), lambda i: (i, 0)),
                  pl.BlockSpec((GRID_W, n2), lambda i: (jnp.minimum((i + 1) * hb, n_hblocks - 1), 0)),
                  _resident((16, n2)), _resident((1, n2)), _resident((dff, d)),
                  pl.BlockSpec((tm, d), lambda i: (i, 0)),
                  pl.BlockSpec((1, 1, d), lambda i: (i // tiles_per_seq, 0, 0)),
                  _resident((1, d))],
        out_specs=pl.BlockSpec((tm, d), lambda i: (i, 0)),
        out_shape=jax.ShapeDtypeStruct((tokens, d), F32),
        scratch_shapes=[pltpu.VMEM((tm, dff), BF16)],
        compiler_params=_cparams(1),
        name="ffn_conv_down",
    )(z, z, z, cw_pad, conv_b.reshape(1, n2), wo, x2d, g2, final_g.reshape(1, d))


def _pick_tile(seq_len, want):
    tm = min(want, seq_len)
    assert seq_len % tm == 0
    return tm


def kernel(x, c, ctx, c_ctx, ada_w, ada_b, norm_g, final_g, e_w_in, e_conv_w, e_conv_b, e_ln_g, e_ln_b, s5_a_re, s5_a_im, s5_log_step, s5_b_re, s5_b_im, s5_c_re, s5_c_im, s5_d, s5_glu_w, s5_glu_b, e_w_out, o_w_in, o_ln_g, o_ln_b, o_sgu_w, o_sgu_b, o_w_out, f_w_in, f_conv_w, f_conv_b, f_w_out):
    bn, seq_len, d = x.shape
    ctx_len = ctx.shape[1]
    depth = ada_w.shape[0]
    assert depth == 2 and bn == 2, "block structure is specialised to depth 2, batch 2"
    tokens = bn * seq_len
    cw = e_conv_w.shape[2]
    sw = s5_d.shape[1]
    g_dim = sw // SSM_P
    t = T_CHUNK
    nc = seq_len // t
    nc_ctx = ctx_len // t
    assert seq_len % t == 0 and ctx_len % t == 0 and seq_len % GRID_W == 0

    x2d = x.reshape(tokens, d)

    s_rows = jnp.zeros((SUBLANES, d), F32).at[:bn].set(c).at[bn].set(c_ctx)
    mods = _ada_call(s_rows, ada_w, ada_b)

    def lat_mod(layer, k):
        return mods[layer, :bn, k * d:(k + 1) * d].reshape(bn, 1, d)

    li = 0
    w_in = e_w_in[li]
    wv = w_in[:, :cw].astype(BF16)
    wg = w_in[:, cw:2 * cw].astype(BF16)
    wut = w_in[:, 2 * cw:].T.astype(BF16)
    toep, bs, wst, coef = _s5_prep_call(s5_a_re[li], s5_a_im[li], s5_log_step[li], s5_b_re[li],
                                        s5_b_im[li], s5_c_re[li], s5_c_im[li], s5_d[li])

    g_row = norm_g[0, 0].reshape(1, d)
    ctx_shift = mods[0, bn:bn + 1, 0:d]
    ctx_scale = mods[0, bn:bn + 1, d:2 * d]
    uct = _ctx_in_call(ctx.reshape(bn * ctx_len, d), g_row, ctx_shift, ctx_scale, wut)
    uc4 = uct.reshape(g_dim, SSM_P, bn * nc_ctx, t)
    rows_ctx = -(-(bn * nc_ctx) // SUBLANES) * SUBLANES
    uc4 = jnp.pad(uc4, ((0, 0), (0, 0), (0, rows_ctx - bn * nc_ctx), (0, 0)))
    zero_h0 = jnp.zeros((g_dim, SUBLANES, LANES), F32)
    (h0s,) = _s5_call(uc4, bs, coef, zero_h0, nb=bn, nc=nc_ctx)

    tm_in = _pick_tile(seq_len, 512)
    z, ut = _even_in_call(x2d, g_row, lat_mod(0, 0), lat_mod(0, 1), wv, wg, wut, seq_len, tm_in)
    yc = _conv_call(z, e_conv_w[li], e_conv_b[li], e_ln_g[li], e_ln_b[li], seq_len,
                    _pick_tile(seq_len, 256))
    u4 = ut.reshape(g_dim, SSM_P, bn * nc, t)
    y4, _ = _s5_call(u4, bs, coef, h0s, toep, wst, nb=bn, nc=nc)
    yt = y4.reshape(sw, tokens)
    w_out = e_w_out[li]
    h = _even_out_call(yc, yt, s5_glu_w[li].T.astype(BF16), s5_glu_b[li].reshape(sw, 1),
                       w_out[:cw].astype(BF16), w_out[cw:].astype(BF16), x2d, lat_mod(0, 2),
                       seq_len, _pick_tile(seq_len, 512))

    def conv_ffn(h, layer, final_norm):
        zf = _ffn_up_call(h, norm_g[layer, 1].reshape(1, d), lat_mod(layer, 3), lat_mod(layer, 4),
                          f_w_in[layer].astype(BF16), seq_len, _pick_tile(seq_len, 1024), 1024)
        return _ffn_down_call(zf, f_conv_w[layer].reshape(9, -1), f_conv_b[layer],
                              f_w_out[layer].astype(BF16), h, lat_mod(layer, 5), final_g,
                              seq_len, _pick_tile(seq_len, 256), final_norm)

    h = conv_ffn(h, 0, False)

    li = 0
    w = o_w_in.shape[2] // 2
    h = _odd_call(h, norm_g[1, 0].reshape(1, d), lat_mod(1, 0), lat_mod(1, 1),
                  o_w_in[li][:, :w].astype(BF16), o_w_in[li][:, w:].astype(BF16),
                  o_ln_g[li].reshape(1, w), o_ln_b[li].reshape(1, w),
                  o_sgu_w[li].astype(BF16), o_sgu_b[li].reshape(SGU_HEADS, SGU_CHUNK, 1),
                  o_w_out[li].astype(BF16), lat_mod(1, 2), seq_len, _pick_tile(seq_len, 256))
    h = conv_ffn(h, 1, True)
    return h.reshape(bn, seq_len, d)
```

```python
import functools
import math

import jax
import jax.numpy as jnp
from jax import lax
from jax.experimental import pallas as pl
from jax.experimental.pallas import tpu as pltpu

F32 = jnp.float32
BF16 = jnp.bfloat16
HIGHEST = lax.Precision.HIGHEST

RMS_EPS = 1e-6
LN_EPS = 1e-5
GRID_W = 64
CONV_K = 31
CONV_HALO = 16
SSM_P = 16
SSM_N = 64
SGU_CHUNK = 128
SGU_HEADS = 8
T_CHUNK = 128
LANES = 128
SUBLANES = 8
VMEM_LIMIT_BYTES = 56 * 1024 * 1024


def _cparams(ngrid):
    return pltpu.CompilerParams(dimension_semantics=("arbitrary",) * ngrid,
                                vmem_limit_bytes=VMEM_LIMIT_BYTES)


def _resident(shape):
    nd = len(shape)
    return pl.BlockSpec(shape, lambda *_: (0,) * nd, pipeline_mode=pl.Buffered(1))


def _sigmoid(x):
    return 1.0 / (1.0 + jnp.exp(-x))


def _silu(x):
    return x * _sigmoid(x)


def _gelu(x):
    return 0.5 * x * (1.0 + jnp.tanh(math.sqrt(2.0 / math.pi) * (x + 0.044715 * (x * x * x))))


def _norm_mod(x, g, shift, scale):
    ms = jnp.mean(x * x, axis=-1, keepdims=True)
    y = x * lax.rsqrt(ms + RMS_EPS) * g
    return y * (1.0 + scale) + shift


def _layer_norm(x, g, b):
    mu = jnp.mean(x, axis=-1, keepdims=True)
    xc = x - mu
    var = jnp.mean(xc * xc, axis=-1, keepdims=True)
    return xc * lax.rsqrt(var + LN_EPS) * g + b


def _ada_body(s_ref, w_ref, b_ref, o_ref):
    s = _silu(s_ref[...])
    o_ref[0] = jnp.dot(s, w_ref[0], preferred_element_type=F32, precision=HIGHEST) + b_ref[0]


def _ada_call(s_rows, ada_w, ada_b):
    depth, d, n = ada_w.shape
    tn = 1024
    return pl.pallas_call(
        _ada_body,
        grid=(depth, n // tn),
        in_specs=[pl.BlockSpec((SUBLANES, d), lambda l, j: (0, 0)),
                  pl.BlockSpec((1, d, tn), lambda l, j: (l, 0, j)),
                  pl.BlockSpec((1, 1, tn), lambda l, j: (l, 0, j))],
        out_specs=pl.BlockSpec((1, SUBLANES, tn), lambda l, j: (l, 0, j)),
        out_shape=jax.ShapeDtypeStruct((depth, SUBLANES, n), F32),
        compiler_params=_cparams(2),
        name="ada_mod",
    )(s_rows, ada_w, ada_b.reshape(depth, 1, n))


def _even_in_body(x_ref, g_ref, sh_ref, sc_ref, wv_ref, wg_ref, wut_ref, z_ref, ut_ref):
    hn = _norm_mod(x_ref[...], g_ref[...], sh_ref[0], sc_ref[0]).astype(BF16)
    av = jnp.dot(hn, wv_ref[...], preferred_element_type=F32)
    ag = jnp.dot(hn, wg_ref[...], preferred_element_type=F32)
    z_ref[...] = av * _sigmoid(ag)
    ut_ref[...] = lax.dot_general(wut_ref[...], hn, (((1,), (1,)), ((), ())),
                                  preferred_element_type=F32)


def _even_in_call(x2d, g, shift, scale, wv, wg, wut, seq_len, tm):
    tokens, d = x2d.shape
    cw = wv.shape[1]
    sw = wut.shape[0]
    tiles_per_seq = seq_len // tm
    mod_spec = pl.BlockSpec((1, 1, d), lambda i: (i // tiles_per_seq, 0, 0))
    return pl.pallas_call(
        _even_in_body,
        grid=(tokens // tm,),
        in_specs=[pl.BlockSpec((tm, d), lambda i: (i, 0)),
                  _resident((1, d)), mod_spec, mod_spec,
                  _resident((d, cw)), _resident((d, cw)), _resident((sw, d))],
        out_specs=[pl.BlockSpec((tm, cw), lambda i: (i, 0)),
                   pl.BlockSpec((sw, tm), lambda i: (0, i))],
        out_shape=[jax.ShapeDtypeStruct((tokens, cw), F32),
                   jax.ShapeDtypeStruct((sw, tokens), F32)],
        compiler_params=_cparams(1),
        name="even_in_proj",
    )(x2d, g, shift, scale, wv, wg, wut)


def _ctx_in_body(x_ref, g_ref, sh_ref, sc_ref, wut_ref, ut_ref):
    hn = _norm_mod(x_ref[...], g_ref[...], sh_ref[...], sc_ref[...]).astype(BF16)
    ut_ref[...] = lax.dot_general(wut_ref[...], hn, (((1,), (1,)), ((), ())),
                                  preferred_element_type=F32)


def _ctx_in_call(c2d, g, shift, scale, wut):
    tokens, d = c2d.shape
    sw = wut.shape[0]
    return pl.pallas_call(
        _ctx_in_body,
        out_shape=jax.ShapeDtypeStruct((sw, tokens), F32),
        compiler_params=pltpu.CompilerParams(vmem_limit_bytes=VMEM_LIMIT_BYTES),
        name="ctx_in_proj",
    )(c2d, g, shift, scale, wut)


CONV_ROWS = 64


def _conv_body(zp_ref, zm_ref, zn_ref, w_ref, b_ref, g_ref, beta_ref, o_ref, buf_ref, acc_ref,
               *, tl, tiles_per_seq):
    i = pl.program_id(0)
    first = (i % tiles_per_seq) == 0
    last = (i % tiles_per_seq) == tiles_per_seq - 1
    buf_ref[0:CONV_HALO, :] = jnp.where(first, 0.0, zp_ref[...])
    buf_ref[CONV_HALO:CONV_HALO + tl, :] = zm_ref[...]
    buf_ref[CONV_HALO + tl:2 * CONV_HALO + tl, :] = jnp.where(last, 0.0, zn_ref[...])

    cw = o_ref.shape[1]
    n_rb = tl // CONV_ROWS
    n_cb = cw // LANES
    rows = CONV_ROWS + SUBLANES

    def block(idx, carry):
        rb = idx // n_cb
        cb = idx % n_cb
        r0 = pl.multiple_of(rb * CONV_ROWS, CONV_ROWS)
        c0 = pl.multiple_of(cb * LANES, LANES)
        acc = jnp.zeros((CONV_ROWS, LANES), F32)
        for r in range(SUBLANES):
            part = None
            for a in range(4):
                k = SUBLANES * a + r - 1
                if k < 0 or k >= CONV_K:
                    continue
                term = buf_ref[pl.ds(r0 + SUBLANES * a, rows), pl.ds(c0, LANES)] * \
                    w_ref[pl.ds(k, 1), pl.ds(c0, LANES)]
                part = term if part is None else part + term
            if r:
                part = pltpu.roll(part, rows - r, axis=0)
            acc = acc + part[:CONV_ROWS]
        acc_ref[pl.ds(r0, CONV_ROWS), pl.ds(c0, LANES)] = acc + b_ref[:, pl.ds(c0, LANES)]
        return carry

    lax.fori_loop(0, n_rb * n_cb, block, 0)

    ln_rows = 128

    def ln_block(rb, carry):
        r0 = pl.multiple_of(rb * ln_rows, ln_rows)
        y = _layer_norm(acc_ref[pl.ds(r0, ln_rows), :], g_ref[...], beta_ref[...])
        o_ref[pl.ds(r0, ln_rows), :] = _silu(y).astype(o_ref.dtype)
        return carry

    lax.fori_loop(0, tl // ln_rows, ln_block, 0)


def _conv_call(z2d, conv_w, conv_b, ln_g, ln_b, seq_len, tl):
    tokens, cw = z2d.shape
    tiles_per_seq = seq_len // tl
    hb = tl // CONV_HALO
    n_hblocks = tokens // CONV_HALO
    w_pad = jnp.pad(conv_w, ((0, 32 - CONV_K), (0, 0)))
    body = functools.partial(_conv_body, tl=tl, tiles_per_seq=tiles_per_seq)
    return pl.pallas_call(
        body,
        grid=(tokens // tl,),
        in_specs=[pl.BlockSpec((CONV_HALO, cw), lambda i: (jnp.maximum(i * hb - 1, 0), 0)),
                  pl.BlockSpec((tl, cw), lambda i: (i, 0)),
                  pl.BlockSpec((CONV_HALO, cw), lambda i: (jnp.minimum((i + 1) * hb, n_hblocks - 1), 0)),
                  _resident((32, cw)), _resident((1, cw)), _resident((1, cw)), _resident((1, cw))],
        out_specs=pl.BlockSpec((tl, cw), lambda i: (i, 0)),
        out_shape=jax.ShapeDtypeStruct((tokens, cw), BF16),
        scratch_shapes=[pltpu.VMEM((tl + 2 * CONV_HALO, cw), F32), pltpu.VMEM((tl, cw), F32)],
        compiler_params=_cparams(1),
        name="conformer_conv",
    )(z2d, z2d, z2d, w_pad, conv_b.reshape(1, cw), ln_g.reshape(1, cw), ln_b.reshape(1, cw))


def _s5_prep_body(ar_row_ref, ai_row_ref, ar_col_ref, ai_col_ref, ls_ref,
                  bt_re_ref, bt_im_ref, c_re_ref, c_im_ref, ct_re_ref, ct_im_ref, dpq_ref,
                  toep_ref, bs_ref, wst_ref, coef_ref, kv_ref):
    t = T_CHUNK
    n = SSM_N
    p_dim = SSM_P
    lane2 = lax.broadcasted_iota(jnp.int32, (1, 2 * t), 1)
    lane1 = lax.broadcasted_iota(jnp.int32, (1, t), 1).astype(F32)
    sub1 = lax.broadcasted_iota(jnp.int32, (t, 1), 0).astype(F32)

    bs_ref[...] = jnp.zeros(bs_ref.shape, bs_ref.dtype)
    wst_ref[...] = jnp.zeros(wst_ref.shape, wst_ref.dtype)
    coef_ref[...] = jnp.zeros(coef_ref.shape, coef_ref.dtype)

    kv = dpq_ref[0] * (lane2 == t).astype(F32)
    for d in range(2):
        dt = jnp.exp(ls_ref[0, d])
        ar = ar_row_ref[0, d]
        ai = ai_row_ref[0, d]
        lam_row = ar * dt
        ang_row = ai * dt
        lam_col = ar_col_ref[0, d] * dt
        ang_col = ai_col_ref[0, d] * dt
        mag = jnp.exp(lam_row)
        ab_re = mag * jnp.cos(ang_row)
        ab_im = mag * jnp.sin(ang_row)
        den = ar * ar + ai * ai
        num_re = ab_re - 1.0
        f_re = (num_re * ar + ab_im * ai) / den
        f_im = (ab_im * ar - num_re * ai) / den
        bbt_re = f_re * bt_re_ref[0, d] - f_im * bt_im_ref[0, d]
        bbt_im = f_re * bt_im_ref[0, d] + f_im * bt_re_ref[0, d]

        if d == 0:
            lag2 = jnp.maximum(lane2 - t, 0).astype(F32)
            msk2 = (lane2 >= t).astype(F32)
        else:
            lag2 = jnp.maximum(t - lane2, 0).astype(F32)
            msk2 = (lane2 <= t).astype(F32)
        pm = jnp.exp(lam_col * lag2) * msk2
        pt_re = pm * jnp.cos(ang_col * lag2)
        pt_im = pm * jnp.sin(ang_col * lag2)
        cb_re = jnp.concatenate(
            [c_re_ref[0, d, pl.ds(p, 1), :] * bbt_re - c_im_ref[0, d, pl.ds(p, 1), :] * bbt_im
             for p in range(p_dim)], axis=0)
        cb_im = jnp.concatenate(
            [c_re_ref[0, d, pl.ds(p, 1), :] * bbt_im + c_im_ref[0, d, pl.ds(p, 1), :] * bbt_re
             for p in range(p_dim)], axis=0)
        kv = kv + (jnp.dot(cb_re, pt_re, preferred_element_type=F32, precision=HIGHEST)
                   - jnp.dot(cb_im, pt_im, preferred_element_type=F32, precision=HIGHEST))

        lag_col = (t - 1.0 - sub1) if d == 0 else sub1
        pw_m = jnp.exp(lag_col * lam_row)
        pw_re = pw_m * jnp.cos(lag_col * ang_row)
        pw_im = pw_m * jnp.sin(lag_col * ang_row)
        for q in range(p_dim):
            br = bbt_re[q:q + 1, :]
            bi = bbt_im[q:q + 1, :]
            bs_ref[0, q * t:(q + 1) * t, (2 * d) * LANES:(2 * d) * LANES + n] = \
                (pw_re * br - pw_im * bi).astype(bs_ref.dtype)
            bs_ref[0, q * t:(q + 1) * t, (2 * d + 1) * LANES:(2 * d + 1) * LANES + n] = \
                (pw_re * bi + pw_im * br).astype(bs_ref.dtype)

        lag_row = (lane1 + 1.0) if d == 0 else (t - lane1)
        qm = jnp.exp(lam_col * lag_row)
        q_re = qm * jnp.cos(ang_col * lag_row)
        q_im = qm * jnp.sin(ang_col * lag_row)
        for p in range(p_dim):
            cr = ct_re_ref[0, d, :, p:p + 1]
            ci = ct_im_ref[0, d, :, p:p + 1]
            wst_ref[0, (2 * d) * LANES:(2 * d) * LANES + n, p * t:(p + 1) * t] = \
                (cr * q_re - ci * q_im).astype(wst_ref.dtype)
            wst_ref[0, (2 * d + 1) * LANES:(2 * d + 1) * LANES + n, p * t:(p + 1) * t] = \
                (-(cr * q_im + ci * q_re)).astype(wst_ref.dtype)

        tt = float(t)
        at_m = jnp.exp(lam_row * tt)
        coef_ref[0, 2 * d:2 * d + 1, 0:n] = at_m * jnp.cos(ang_row * tt)
        coef_ref[0, 2 * d + 1:2 * d + 2, 0:n] = at_m * jnp.sin(ang_row * tt)

    kv_ref[...] = kv

    def toep_block(p, carry):
        kv_p = kv_ref[pl.ds(pl.multiple_of(p * p_dim, p_dim), p_dim), :]
        for q in range(p_dim):
            row = jnp.broadcast_to(kv_p[q:q + 1, :], (t, 2 * t))
            rolled = pltpu.roll(row, 0, axis=1, stride=1, stride_axis=0)
            toep_ref[0, q * t:(q + 1) * t, pl.ds(pl.multiple_of(p * t, t), t)] = \
                rolled[:, t:].astype(toep_ref.dtype)
        return carry

    lax.fori_loop(0, p_dim, toep_block, 0)


def _s5_prep_call(a_re, a_im, log_step, b_re, b_im, c_re, c_im, d_skip):
    g_dim = a_re.shape[1]
    n, p_dim, t = SSM_N, SSM_P, T_CHUNK
    gm = lambda v: jnp.swapaxes(v, 0, 1)
    ar, ai = gm(a_re), gm(a_im)
    dpq = (jnp.eye(p_dim, dtype=F32)[None] * d_skip.reshape(g_dim, p_dim, 1)).reshape(g_dim, p_dim * p_dim, 1)
    args = (ar[:, :, None, :], ai[:, :, None, :], ar[:, :, :, None], ai[:, :, :, None],
            gm(log_step)[:, :, None, None],
            jnp.swapaxes(gm(b_re), 2, 3), jnp.swapaxes(gm(b_im), 2, 3),
            gm(c_re), gm(c_im), jnp.swapaxes(gm(c_re), 2, 3), jnp.swapaxes(gm(c_im), 2, 3), dpq)

    def spec(v):
        blk = (1,) + v.shape[1:]
        nd = v.ndim
        return pl.BlockSpec(blk, lambda g: (g,) + (0,) * (nd - 1))

    out_shapes = [jax.ShapeDtypeStruct((g_dim, p_dim * t, p_dim * t), BF16),
                  jax.ShapeDtypeStruct((g_dim, p_dim * t, 4 * LANES), BF16),
                  jax.ShapeDtypeStruct((g_dim, 4 * LANES, p_dim * t), BF16),
                  jax.ShapeDtypeStruct((g_dim, SUBLANES, LANES), F32)]
    return pl.pallas_call(
        _s5_prep_body,
        grid=(g_dim,),
        in_specs=[spec(v) for v in args],
        out_specs=[spec(s) for s in out_shapes],
        out_shape=out_shapes,
        scratch_shapes=[pltpu.VMEM((p_dim * p_dim, 2 * t), F32)],
        compiler_params=_cparams(1),
        name="s5_prep",
    )(*args)


def _s5_body(*refs, nb, nc, with_y):
    if with_y:
        u_ref, bs_ref, coef_ref, h0_ref, toep_ref, wst_ref, y_ref, fin_ref, s_scr, h_scr = refs
    else:
        u_ref, bs_ref, coef_ref, h0_ref, fin_ref, s_scr = refs
    p_dim = u_ref.shape[1]
    t = T_CHUNK
    u = jnp.concatenate([u_ref[0, q] for q in range(p_dim)], axis=1).astype(BF16)
    s_scr[...] = jnp.dot(u, bs_ref[0], preferred_element_type=F32)
    if with_y:
        h_scr[...] = jnp.zeros(h_scr.shape, h_scr.dtype)

    af_re, af_im = coef_ref[0, 0:1, :], coef_ref[0, 1:2, :]
    ab_re, ab_im = coef_ref[0, 2:3, :], coef_ref[0, 3:4, :]
    init = tuple(h0_ref[0, r:r + 1, :] for r in range(4 * nb))

    blk = SUBLANES if nc % SUBLANES == 0 else nc

    def block(k, carry):
        new = list(carry)
        for b in range(nb):
            f0 = b * nc + blk * k
            b0 = b * nc + nc - blk - blk * k
            if blk == SUBLANES:
                f0 = pl.multiple_of(f0, SUBLANES)
                b0 = pl.multiple_of(b0, SUBLANES)
            sf_re = s_scr[pl.ds(f0, blk), 0:LANES]
            sf_im = s_scr[pl.ds(f0, blk), LANES:2 * LANES]
            sb_re = s_scr[pl.ds(b0, blk), 2 * LANES:3 * LANES]
            sb_im = s_scr[pl.ds(b0, blk), 3 * LANES:4 * LANES]
            fr, fi, br, bi = new[4 * b:4 * b + 4]
            hf_re, hf_im, hb_re, hb_im = [], [], [], []
            for j in range(blk):
                jb = blk - 1 - j
                hf_re.append(fr)
                hf_im.append(fi)
                hb_re.insert(0, br)
                hb_im.insert(0, bi)
                fr, fi = (af_re * fr - af_im * fi + sf_re[j:j + 1],
                          af_re * fi + af_im * fr + sf_im[j:j + 1])
                br, bi = (ab_re * br - ab_im * bi + sb_re[jb:jb + 1],
                          ab_re * bi + ab_im * br + sb_im[jb:jb + 1])
            if with_y:
                h_scr[pl.ds(f0, blk), 0:LANES] = jnp.concatenate(hf_re, axis=0)
                h_scr[pl.ds(f0, blk), LANES:2 * LANES] = jnp.concatenate(hf_im, axis=0)
                h_scr[pl.ds(b0, blk), 2 * LANES:3 * LANES] = jnp.concatenate(hb_re, axis=0)
                h_scr[pl.ds(b0, blk), 3 * LANES:4 * LANES] = jnp.concatenate(hb_im, axis=0)
            new[4 * b:4 * b + 4] = [fr, fi, br, bi]
        return tuple(new)

    if nc == blk:
        fin = block(0, init)
    else:
        fin = lax.fori_loop(0, nc // blk, block, init)
    fin_ref[0] = jnp.concatenate(list(fin) + [jnp.zeros((1, LANES), F32)] * (SUBLANES - 4 * nb), axis=0) \
        if 4 * nb < SUBLANES else jnp.concatenate(list(fin), axis=0)

    if with_y:
        y = jnp.dot(u, toep_ref[0], preferred_element_type=F32)
        y = y + jnp.dot(h_scr[...].astype(BF16), wst_ref[0], preferred_element_type=F32)
        y = _gelu(y)
        for p in range(p_dim):
            y_ref[0, p] = y[:, p * t:(p + 1) * t]


def _s5_call(u4, bs, coef, h0, toep=None, wst=None, *, nb, nc):
    g_dim, p_dim, rows, t = u4.shape
    with_y = toep is not None
    body = functools.partial(_s5_body, nb=nb, nc=nc, with_y=with_y)

    def spec(v):
        blk = (1,) + v.shape[1:]
        nd = v.ndim
        return pl.BlockSpec(blk, lambda g: (g,) + (0,) * (nd - 1))

    fin_shape = jax.ShapeDtypeStruct((g_dim, SUBLANES, LANES), F32)
    args = [u4, bs, coef, h0]
    scratch = [pltpu.VMEM((rows, 4 * LANES), F32)]
    if with_y:
        args += [toep, wst]
        out_shape = [jax.ShapeDtypeStruct(u4.shape, F32), fin_shape]
        scratch.append(pltpu.VMEM((rows, 4 * LANES), F32))
    else:
        out_shape = [fin_shape]
    outs = pl.pallas_call(
        body,
        grid=(g_dim,),
        in_specs=[spec(v) for v in args],
        out_specs=[spec(s) for s in out_shape],
        out_shape=out_shape,
        scratch_shapes=scratch,
        compiler_params=_cparams(1),
        name="s5_chunks" if with_y else "s5_ctx_states",
    )(*args)
    return outs


def _even_out_body(yc_ref, yt_ref, gwt_ref, gb_ref, woa_ref, wob_ref, x_ref, g1_ref, o_ref):
    yt = yt_ref[...]
    gl = jnp.dot(gwt_ref[...], yt.astype(BF16), preferred_element_type=F32) + gb_ref[...]
    ys = (yt * _sigmoid(gl)).astype(BF16)
    mix = jnp.dot(yc_ref[...], woa_ref[...], preferred_element_type=F32)
    mix = mix + jnp.dot(ys.T, wob_ref[...], preferred_element_type=F32)
    o_ref[...] = x_ref[...] + g1_ref[0] * mix


def _even_out_call(yc, yt, gwt, gb, woa, wob, x2d, g1, seq_len, tm):
    tokens, d = x2d.shape
    cw = yc.shape[1]
    sw = yt.shape[0]
    tiles_per_seq = seq_len // tm
    return pl.pallas_call(
        _even_out_body,
        grid=(tokens // tm,),
        in_specs=[pl.BlockSpec((tm, cw), lambda i: (i, 0)),
                  pl.BlockSpec((sw, tm), lambda i: (0, i)),
                  _resident((sw, sw)), _resident((sw, 1)),
                  _resident((cw, d)), _resident((sw, d)),
                  pl.BlockSpec((tm, d), lambda i: (i, 0)),
                  pl.BlockSpec((1, 1, d), lambda i: (i // tiles_per_seq, 0, 0))],
        out_specs=pl.BlockSpec((tm, d), lambda i: (i, 0)),
        out_shape=jax.ShapeDtypeStruct((tokens, d), F32),
        compiler_params=_cparams(1),
        name="even_out_proj",
    )(yc, yt, gwt, gb, woa, wob, x2d, g1)


def _odd_body(x_ref, g_ref, sh_ref, sc_ref, wu_ref, wv_ref, lg_ref, lb_ref, sw_ref, sb_ref,
              wo_ref, g1_ref, o_ref, gated_ref):
    x = x_ref[...]
    tm = x.shape[0]
    hn = _norm_mod(x, g_ref[...], sh_ref[0], sc_ref[0]).astype(BF16)
    v = _gelu(jnp.dot(hn, wv_ref[...], preferred_element_type=F32))
    v = _layer_norm(v, lg_ref[...], lb_ref[...]).astype(BF16)
    u = _gelu(jnp.dot(hn, wu_ref[...], preferred_element_type=F32))
    hd = v.shape[1] // SGU_HEADS
    for ch in range(tm // SGU_CHUNK):
        r0 = ch * SGU_CHUNK
        for h in range(SGU_HEADS):
            s = jnp.dot(sw_ref[h], v[r0:r0 + SGU_CHUNK, h * hd:(h + 1) * hd],
                        preferred_element_type=F32) + sb_ref[h]
            gated_ref[r0:r0 + SGU_CHUNK, h * hd:(h + 1) * hd] = \
                (u[r0:r0 + SGU_CHUNK, h * hd:(h + 1) * hd] * s).astype(BF16)
    mix = jnp.dot(gated_ref[...], wo_ref[...], preferred_element_type=F32)
    o_ref[...] = x + g1_ref[0] * mix


def _odd_call(x2d, g, shift, scale, wu, wv, ln_g, ln_b, sgu_w, sgu_b, wo, g1, seq_len, tm):
    tokens, d = x2d.shape
    w = wu.shape[1]
    tiles_per_seq = seq_len // tm
    mod_spec = pl.BlockSpec((1, 1, d), lambda i: (i // tiles_per_seq, 0, 0))
    return pl.pallas_call(
        _odd_body,
        grid=(tokens // tm,),
        in_specs=[pl.BlockSpec((tm, d), lambda i: (i, 0)),
                  _resident((1, d)), mod_spec, mod_spec,
                  _resident((d, w)), _resident((d, w)), _resident((1, w)), _resident((1, w)),
                  _resident(sgu_w.shape), _resident(sgu_b.shape),
                  _resident((w, d)), mod_spec],
        out_specs=pl.BlockSpec((tm, d), lambda i: (i, 0)),
        out_shape=jax.ShapeDtypeStruct((tokens, d), F32),
        scratch_shapes=[pltpu.VMEM((tm, w), BF16)],
        compiler_params=_cparams(1),
        name="odd_gmlp",
    )(x2d, g, shift, scale, wu, wv, ln_g, ln_b, sgu_w, sgu_b, wo, g1)


def _ffn_up_body(x_ref, g_ref, sh_ref, sc_ref, w_ref, z_ref, hn_ref):
    @pl.when(pl.program_id(1) == 0)
    def _():
        hn_ref[...] = _norm_mod(x_ref[...], g_ref[...], sh_ref[0], sc_ref[0]).astype(BF16)

    z_ref[...] = jnp.dot(hn_ref[...], w_ref[...], preferred_element_type=F32).astype(z_ref.dtype)


def _ffn_up_call(x2d, g, shift, scale, w, seq_len, tm, tn):
    tokens, d = x2d.shape
    n = w.shape[1]
    tiles_per_seq = seq_len // tm
    mod_spec = pl.BlockSpec((1, 1, d), lambda i, j: (i // tiles_per_seq, 0, 0))
    return pl.pallas_call(
        _ffn_up_body,
        grid=(tokens // tm, n // tn),
        in_specs=[pl.BlockSpec((tm, d), lambda i, j: (i, 0)),
                  _resident((1, d)), mod_spec, mod_spec,
                  pl.BlockSpec((d, tn), lambda i, j: (0, j))],
        out_specs=pl.BlockSpec((tm, tn), lambda i, j: (i, j)),
        out_shape=jax.ShapeDtypeStruct((tokens, n), BF16),
        scratch_shapes=[pltpu.VMEM((tm, d), BF16)],
        compiler_params=_cparams(2),
        name="ffn_up_proj",
    )(x2d, g, shift, scale, w)


FFN_PIECE = 256


def _ffn_down_body(zp_ref, zm_ref, zn_ref, cw_ref, cb_ref, wo_ref, x_ref, g2_ref, fg_ref, o_ref,
                   act0_ref, act1_ref, acc_ref, *, tm, tiles_per_seq, n_tiles, final_norm):
    i = pl.program_id(0)
    it = jnp.minimum(i, n_tiles - 1)
    first = (it % tiles_per_seq) == 0
    last = (it % tiles_per_seq) == tiles_per_seq - 1
    dff = act0_ref.shape[1]
    d = o_ref.shape[1]
    n_rb = tm // GRID_W
    n_pieces = d // FFN_PIECE
    cb_per_piece = dff // LANES // n_pieces
    row_id = lax.broadcasted_iota(jnp.int32, (GRID_W, LANES), 0)
    has_left = row_id != 0
    has_right = row_id != GRID_W - 1
    top_ok = jnp.where(first, 0.0, 1.0)
    bot_ok = jnp.where(last, 0.0, 1.0)

    def grid_row(rb, c0):
        if rb < 0:
            return zp_ref[:, pl.ds(c0, LANES)].astype(F32) * top_ok
        if rb >= n_rb:
            return zn_ref[:, pl.ds(c0, LANES)].astype(F32) * bot_ok
        return zm_ref[rb * GRID_W:(rb + 1) * GRID_W, pl.ds(c0, LANES)].astype(F32)

    def conv_rows(rb, c0, dep):
        up, mid, dn = grid_row(rb - 1, c0), grid_row(rb, c0), grid_row(rb + 1, c0)
        w = lambda k: cw_ref[pl.ds(k, 1), pl.ds(c0, LANES)]
        left = up * w(0) + mid * w(3) + dn * w(6)
        cent = up * w(1) + mid * w(4) + dn * w(7)
        right = up * w(2) + mid * w(5) + dn * w(8)
        left = jnp.where(has_left, pltpu.roll(left, 1, axis=0), 0.0)
        right = jnp.where(has_right, pltpu.roll(right, GRID_W - 1, axis=0), 0.0)
        return cent + left + right + (cb_ref[:, pl.ds(c0, LANES)] + dep)

    @pl.when(i == 0)
    def _():
        act1_ref[...] = jnp.zeros(act1_ref.shape, act1_ref.dtype)

    def run(act_w, act_r):
        def piece(n, carry):
            cn = pl.multiple_of(n * FFN_PIECE, FFN_PIECE)
            n_slices = cb_per_piece * n_rb
            kc = dff // n_slices
            deps = list(carry)
            for j in range(cb_per_piece):
                ca = pl.multiple_of((n * cb_per_piece + j) * LANES, LANES)
                cg = pl.multiple_of(dff + (n * cb_per_piece + j) * LANES, LANES)
                for rb in range(n_rb):
                    dep = deps.pop(0)
                    a = conv_rows(rb, ca, dep)
                    g = conv_rows(rb, cg, dep)
                    act_w[rb * GRID_W:(rb + 1) * GRID_W, pl.ds(ca, LANES)] = (a * _silu(g)).astype(BF16)
                    m = j * n_rb + rb
                    part = jnp.dot(act_r[:, m * kc:(m + 1) * kc],
                                   wo_ref[m * kc:(m + 1) * kc, pl.ds(cn, FFN_PIECE)],
                                   preferred_element_type=F32)
                    if m == 0:
                        acc_ref[...] = part
                    elif m < n_slices - 1:
                        acc_ref[...] += part
                    else:
                        o_ref[:, pl.ds(cn, FFN_PIECE)] = x_ref[:, pl.ds(cn, FFN_PIECE)] + \
                            g2_ref[0, :, pl.ds(cn, FFN_PIECE)] * (acc_ref[...] + part)
                    deps.append(part[0:1, 0:LANES] * 0.0)
            return tuple(deps)

        zero_row = jnp.zeros((1, LANES), F32)
        lax.fori_loop(0, n_pieces, piece, (zero_row, zero_row))

    @pl.when(i % 2 == 0)
    def _():
        run(act0_ref, act1_ref)

    @pl.when(i % 2 == 1)
    def _():
        run(act1_ref, act0_ref)

    if final_norm:
        y = o_ref[...]
        ms = jnp.mean(y * y, axis=-1, keepdims=True)
        o_ref[...] = y * lax.rsqrt(ms + RMS_EPS) * fg_ref[...]


def _ffn_down_call(z, conv_w9, conv_b, wo, x2d, g2, final_g, seq_len, tm, final_norm):
    tokens, d = x2d.shape
    n2 = z.shape[1]
    dff = n2 // 2
    tiles_per_seq = seq_len // tm
    n_tiles = tokens // tm
    hb = tm // GRID_W
    n_hblocks = tokens // GRID_W
    cw_pad = jnp.pad(conv_w9, ((0, 16 - 9), (0, 0)))
    body = functools.partial(_ffn_down_body, tm=tm, tiles_per_seq=tiles_per_seq, n_tiles=n_tiles,
                             final_norm=final_norm)
    conv_tile = lambda i: jnp.minimum(i, n_tiles - 1)
    out_tile = lambda i: jnp.maximum(i - 1, 0)
    return pl.pallas_call(
        body,
        grid=(n_tiles + 1,),
        in_specs=[pl.BlockSpec((GRID_W, n2), lambda i: (jnp.maximum(conv_tile(i) * hb - 1, 0), 0)),
                  pl.BlockSpec((tm, n2), lambda i: (conv_tile(i), 0)),
                  pl.BlockSpec((GRID_W, n2),
                               lambda i: (jnp.minimum((conv_tile(i) + 1) * hb, n_hblocks - 1), 0)),
                  _resident((16, n2)), _resident((1, n2)), _resident((dff, d)),
                  pl.BlockSpec((tm, d), lambda i: (out_tile(i), 0)),
                  pl.BlockSpec((1, 1, d), lambda i: (out_tile(i) // tiles_per_seq, 0, 0)),
                  _resident((1, d))],
        out_specs=pl.BlockSpec((tm, d), lambda i: (out_tile(i), 0)),
        out_shape=jax.ShapeDtypeStruct((tokens, d), F32),
        scratch_shapes=[pltpu.VMEM((tm, dff), BF16), pltpu.VMEM((tm, dff), BF16),
                        pltpu.VMEM((tm, FFN_PIECE), F32)],
        compiler_params=_cparams(1),
        name="ffn_conv_down",
    )(z, z, z, cw_pad, conv_b.reshape(1, n2), wo, x2d, g2, final_g.reshape(1, d))


def _pick_tile(seq_len, want):
    tm = min(want, seq_len)
    assert seq_len % tm == 0
    return tm


def kernel(x, c, ctx, c_ctx, ada_w, ada_b, norm_g, final_g, e_w_in, e_conv_w, e_conv_b, e_ln_g, e_ln_b, s5_a_re, s5_a_im, s5_log_step, s5_b_re, s5_b_im, s5_c_re, s5_c_im, s5_d, s5_glu_w, s5_glu_b, e_w_out, o_w_in, o_ln_g, o_ln_b, o_sgu_w, o_sgu_b, o_w_out, f_w_in, f_conv_w, f_conv_b, f_w_out):
    bn, seq_len, d = x.shape
    ctx_len = ctx.shape[1]
    depth = ada_w.shape[0]
    assert depth == 2 and bn == 2, "block structure is specialised to depth 2, batch 2"
    tokens = bn * seq_len
    cw = e_conv_w.shape[2]
    sw = s5_d.shape[1]
    g_dim = sw // SSM_P
    t = T_CHUNK
    nc = seq_len // t
    nc_ctx = ctx_len // t
    assert seq_len % t == 0 and ctx_len % t == 0 and seq_len % GRID_W == 0

    x2d = x.reshape(tokens, d)

    s_rows = jnp.zeros((SUBLANES, d), F32).at[:bn].set(c).at[bn].set(c_ctx)
    mods = _ada_call(s_rows, ada_w, ada_b)

    def lat_mod(layer, k):
        return mods[layer, :bn, k * d:(k + 1) * d].reshape(bn, 1, d)

    li = 0
    w_in = e_w_in[li]
    wv = w_in[:, :cw].astype(BF16)
    wg = w_in[:, cw:2 * cw].astype(BF16)
    wut = w_in[:, 2 * cw:].T.astype(BF16)
    toep, bs, wst, coef = _s5_prep_call(s5_a_re[li], s5_a_im[li], s5_log_step[li], s5_b_re[li],
                                        s5_b_im[li], s5_c_re[li], s5_c_im[li], s5_d[li])

    g_row = norm_g[0, 0].reshape(1, d)
    ctx_shift = mods[0, bn:bn + 1, 0:d]
    ctx_scale = mods[0, bn:bn + 1, d:2 * d]
    uct = _ctx_in_call(ctx.reshape(bn * ctx_len, d), g_row, ctx_shift, ctx_scale, wut)
    uc4 = uct.reshape(g_dim, SSM_P, bn * nc_ctx, t)
    rows_ctx = -(-(bn * nc_ctx) // SUBLANES) * SUBLANES
    uc4 = jnp.pad(uc4, ((0, 0), (0, 0), (0, rows_ctx - bn * nc_ctx), (0, 0)))
    zero_h0 = jnp.zeros((g_dim, SUBLANES, LANES), F32)
    (h0s,) = _s5_call(uc4, bs, coef, zero_h0, nb=bn, nc=nc_ctx)

    tm_in = _pick_tile(seq_len, 512)
    z, ut = _even_in_call(x2d, g_row, lat_mod(0, 0), lat_mod(0, 1), wv, wg, wut, seq_len, tm_in)
    yc = _conv_call(z, e_conv_w[li], e_conv_b[li], e_ln_g[li], e_ln_b[li], seq_len,
                    _pick_tile(seq_len, 256))
    u4 = ut.reshape(g_dim, SSM_P, bn * nc, t)
    y4, _ = _s5_call(u4, bs, coef, h0s, toep, wst, nb=bn, nc=nc)
    yt = y4.reshape(sw, tokens)
    w_out = e_w_out[li]
    h = _even_out_call(yc, yt, s5_glu_w[li].T.astype(BF16), s5_glu_b[li].reshape(sw, 1),
                       w_out[:cw].astype(BF16), w_out[cw:].astype(BF16), x2d, lat_mod(0, 2),
                       seq_len, _pick_tile(seq_len, 512))

    def conv_ffn(h, layer, final_norm):
        zf = _ffn_up_call(h, norm_g[layer, 1].reshape(1, d), lat_mod(layer, 3), lat_mod(layer, 4),
                          f_w_in[layer].astype(BF16), seq_len, _pick_tile(seq_len, 1024), 1024)
        return _ffn_down_call(zf, f_conv_w[layer].reshape(9, -1), f_conv_b[layer],
                              f_w_out[layer].astype(BF16), h, lat_mod(layer, 5), final_g,
                              seq_len, _pick_tile(seq_len, 256), final_norm)

    h = conv_ffn(h, 0, False)

    li = 0
    w = o_w_in.shape[2] // 2
    h = _odd_call(h, norm_g[1, 0].reshape(1, d), lat_mod(1, 0), lat_mod(1, 1),
                  o_w_in[li][:, :w].astype(BF16), o_w_in[li][:, w:].astype(BF16),
                  o_ln_g[li].reshape(1, w), o_ln_b[li].reshape(1, w),
                  o_sgu_w[li].astype(BF16), o_sgu_b[li].reshape(SGU_HEADS, SGU_CHUNK, 1),
                  o_w_out[li].astype(BF16), lat_mod(1, 2), seq_len, _pick_tile(seq_len, 256))
    h = conv_ffn(h, 1, True)
    return h.reshape(bn, seq_len, d)
```

```python
import functools
import math

import jax
import jax.numpy as jnp
from jax import lax
from jax.experimental import pallas as pl
from jax.experimental.pallas import tpu as pltpu

F32 = jnp.float32
BF16 = jnp.bfloat16
HIGHEST = lax.Precision.HIGHEST

RMS_EPS = 1e-6
LN_EPS = 1e-5
GRID_W = 64
CONV_K = 31
CONV_HALO = 16
SSM_P = 16
SSM_N = 64
SGU_CHUNK = 128
SGU_HEADS = 8
T_CHUNK = 128
LANES = 128
SUBLANES = 8
VMEM_LIMIT_BYTES = 56 * 1024 * 1024


def _cparams(ngrid):
    return pltpu.CompilerParams(dimension_semantics=("arbitrary",) * ngrid,
                                vmem_limit_bytes=VMEM_LIMIT_BYTES)


def _resident(shape):
    nd = len(shape)
    return pl.BlockSpec(shape, lambda *_: (0,) * nd, pipeline_mode=pl.Buffered(1))


def _sigmoid(x):
    return 1.0 / (1.0 + jnp.exp(-x))


def _silu(x):
    return x * _sigmoid(x)


def _gelu(x):
    return 0.5 * x * (1.0 + jnp.tanh(math.sqrt(2.0 / math.pi) * (x + 0.044715 * (x * x * x))))


def _norm_mod(x, g, shift, scale):
    ms = jnp.mean(x * x, axis=-1, keepdims=True)
    y = x * lax.rsqrt(ms + RMS_EPS) * g
    return y * (1.0 + scale) + shift


def _layer_norm(x, g, b):
    mu = jnp.mean(x, axis=-1, keepdims=True)
    xc = x - mu
    var = jnp.mean(xc * xc, axis=-1, keepdims=True)
    return xc * lax.rsqrt(var + LN_EPS) * g + b


def _ada_body(s_ref, w_ref, b_ref, o_ref):
    s = _silu(s_ref[...])
    o_ref[0] = jnp.dot(s, w_ref[0], preferred_element_type=F32, precision=HIGHEST) + b_ref[0]


def _ada_call(s_rows, ada_w, ada_b):
    depth, d, n = ada_w.shape
    tn = 1024
    return pl.pallas_call(
        _ada_body,
        grid=(depth, n // tn),
        in_specs=[pl.BlockSpec((SUBLANES, d), lambda l, j: (0, 0)),
                  pl.BlockSpec((1, d, tn), lambda l, j: (l, 0, j)),
                  pl.BlockSpec((1, 1, tn), lambda l, j: (l, 0, j))],
        out_specs=pl.BlockSpec((1, SUBLANES, tn), lambda l, j: (l, 0, j)),
        out_shape=jax.ShapeDtypeStruct((depth, SUBLANES, n), F32),
        compiler_params=_cparams(2),
        name="ada_mod",
    )(s_rows, ada_w, ada_b.reshape(depth, 1, n))


def _even_in_body(x_ref, g_ref, sh_ref, sc_ref, wv_ref, wg_ref, wut_ref, z_ref, ut_ref):
    hn = _norm_mod(x_ref[...], g_ref[...], sh_ref[0], sc_ref[0]).astype(BF16)
    av = jnp.dot(hn, wv_ref[...], preferred_element_type=F32)
    ag = jnp.dot(hn, wg_ref[...], preferred_element_type=F32)
    z_ref[...] = av * _sigmoid(ag)
    ut_ref[...] = lax.dot_general(wut_ref[...], hn, (((1,), (1,)), ((), ())),
                                  preferred_element_type=F32)


def _even_in_call(x2d, g, shift, scale, wv, wg, wut, seq_len, tm):
    tokens, d = x2d.shape
    cw = wv.shape[1]
    sw = wut.shape[0]
    tiles_per_seq = seq_len // tm
    mod_spec = pl.BlockSpec((1, 1, d), lambda i: (i // tiles_per_seq, 0, 0))
    return pl.pallas_call(
        _even_in_body,
        grid=(tokens // tm,),
        in_specs=[pl.BlockSpec((tm, d), lambda i: (i, 0)),
                  _resident((1, d)), mod_spec, mod_spec,
                  _resident((d, cw)), _resident((d, cw)), _resident((sw, d))],
        out_specs=[pl.BlockSpec((tm, cw), lambda i: (i, 0)),
                   pl.BlockSpec((sw, tm), lambda i: (0, i))],
        out_shape=[jax.ShapeDtypeStruct((tokens, cw), F32),
                   jax.ShapeDtypeStruct((sw, tokens), F32)],
        compiler_params=_cparams(1),
        name="even_in_proj",
    )(x2d, g, shift, scale, wv, wg, wut)


def _ctx_in_body(x_ref, g_ref, sh_ref, sc_ref, wut_ref, ut_ref):
    hn = _norm_mod(x_ref[...], g_ref[...], sh_ref[...], sc_ref[...]).astype(BF16)
    ut_ref[...] = lax.dot_general(wut_ref[...], hn, (((1,), (1,)), ((), ())),
                                  preferred_element_type=F32)


def _ctx_in_call(c2d, g, shift, scale, wut):
    tokens, d = c2d.shape
    sw = wut.shape[0]
    return pl.pallas_call(
        _ctx_in_body,
        out_shape=jax.ShapeDtypeStruct((sw, tokens), F32),
        compiler_params=pltpu.CompilerParams(vmem_limit_bytes=VMEM_LIMIT_BYTES),
        name="ctx_in_proj",
    )(c2d, g, shift, scale, wut)


CONV_ROWS = 64


def _conv_body(zp_ref, zm_ref, zn_ref, w_ref, b_ref, g_ref, beta_ref, o_ref, buf_ref, acc_ref,
               *, tl, tiles_per_seq):
    i = pl.program_id(0)
    first = (i % tiles_per_seq) == 0
    last = (i % tiles_per_seq) == tiles_per_seq - 1
    buf_ref[0:CONV_HALO, :] = jnp.where(first, 0.0, zp_ref[...])
    buf_ref[CONV_HALO:CONV_HALO + tl, :] = zm_ref[...]
    buf_ref[CONV_HALO + tl:2 * CONV_HALO + tl, :] = jnp.where(last, 0.0, zn_ref[...])

    cw = o_ref.shape[1]
    n_rb = tl // CONV_ROWS
    n_cb = cw // LANES
    rows = CONV_ROWS + SUBLANES

    def block(idx, carry):
        rb = idx // n_cb
        cb = idx % n_cb
        r0 = pl.multiple_of(rb * CONV_ROWS, CONV_ROWS)
        c0 = pl.multiple_of(cb * LANES, LANES)
        acc = jnp.zeros((CONV_ROWS, LANES), F32)
        for r in range(SUBLANES):
            part = None
            for a in range(4):
                k = SUBLANES * a + r - 1
                if k < 0 or k >= CONV_K:
                    continue
                term = buf_ref[pl.ds(r0 + SUBLANES * a, rows), pl.ds(c0, LANES)] * \
                    w_ref[pl.ds(k, 1), pl.ds(c0, LANES)]
                part = term if part is None else part + term
            if r:
                part = pltpu.roll(part, rows - r, axis=0)
            acc = acc + part[:CONV_ROWS]
        acc_ref[pl.ds(r0, CONV_ROWS), pl.ds(c0, LANES)] = acc + b_ref[:, pl.ds(c0, LANES)]
        return carry

    lax.fori_loop(0, n_rb * n_cb, block, 0)

    ln_rows = 128

    def ln_block(rb, carry):
        r0 = pl.multiple_of(rb * ln_rows, ln_rows)
        y = _layer_norm(acc_ref[pl.ds(r0, ln_rows), :], g_ref[...], beta_ref[...])
        o_ref[pl.ds(r0, ln_rows), :] = _silu(y).astype(o_ref.dtype)
        return carry

    lax.fori_loop(0, tl // ln_rows, ln_block, 0)


def _conv_call(z2d, conv_w, conv_b, ln_g, ln_b, seq_len, tl):
    tokens, cw = z2d.shape
    tiles_per_seq = seq_len // tl
    hb = tl // CONV_HALO
    n_hblocks = tokens // CONV_HALO
    w_pad = jnp.pad(conv_w, ((0, 32 - CONV_K), (0, 0)))
    body = functools.partial(_conv_body, tl=tl, tiles_per_seq=tiles_per_seq)
    return pl.pallas_call(
        body,
        grid=(tokens // tl,),
        in_specs=[pl.BlockSpec((CONV_HALO, cw), lambda i: (jnp.maximum(i * hb - 1, 0), 0)),
                  pl.BlockSpec((tl, cw), lambda i: (i, 0)),
                  pl.BlockSpec((CONV_HALO, cw), lambda i: (jnp.minimum((i + 1) * hb, n_hblocks - 1), 0)),
                  _resident((32, cw)), _resident((1, cw)), _resident((1, cw)), _resident((1, cw))],
        out_specs=pl.BlockSpec((tl, cw), lambda i: (i, 0)),
        out_shape=jax.ShapeDtypeStruct((tokens, cw), BF16),
        scratch_shapes=[pltpu.VMEM((tl + 2 * CONV_HALO, cw), F32), pltpu.VMEM((tl, cw), F32)],
        compiler_params=_cparams(1),
        name="conformer_conv",
    )(z2d, z2d, z2d, w_pad, conv_b.reshape(1, cw), ln_g.reshape(1, cw), ln_b.reshape(1, cw))


def _s5_prep_body(ar_row_ref, ai_row_ref, ar_col_ref, ai_col_ref, ls_ref,
                  bt_re_ref, bt_im_ref, c_re_ref, c_im_ref, ct_re_ref, ct_im_ref, dpq_ref,
                  toep_ref, bs_ref, wst_ref, coef_ref, kv_ref):
    t = T_CHUNK
    n = SSM_N
    p_dim = SSM_P
    lane2 = lax.broadcasted_iota(jnp.int32, (1, 2 * t), 1)
    lane1 = lax.broadcasted_iota(jnp.int32, (1, t), 1).astype(F32)
    sub1 = lax.broadcasted_iota(jnp.int32, (t, 1), 0).astype(F32)

    bs_ref[...] = jnp.zeros(bs_ref.shape, bs_ref.dtype)
    wst_ref[...] = jnp.zeros(wst_ref.shape, wst_ref.dtype)
    coef_ref[...] = jnp.zeros(coef_ref.shape, coef_ref.dtype)

    kv = dpq_ref[0] * (lane2 == t).astype(F32)
    for d in range(2):
        dt = jnp.exp(ls_ref[0, d])
        ar = ar_row_ref[0, d]
        ai = ai_row_ref[0, d]
        lam_row = ar * dt
        ang_row = ai * dt
        lam_col = ar_col_ref[0, d] * dt
        ang_col = ai_col_ref[0, d] * dt
        mag = jnp.exp(lam_row)
        ab_re = mag * jnp.cos(ang_row)
        ab_im = mag * jnp.sin(ang_row)
        den = ar * ar + ai * ai
        num_re = ab_re - 1.0
        f_re = (num_re * ar + ab_im * ai) / den
        f_im = (ab_im * ar - num_re * ai) / den
        bbt_re = f_re * bt_re_ref[0, d] - f_im * bt_im_ref[0, d]
        bbt_im = f_re * bt_im_ref[0, d] + f_im * bt_re_ref[0, d]

        if d == 0:
            lag2 = jnp.maximum(lane2 - t, 0).astype(F32)
            msk2 = (lane2 >= t).astype(F32)
        else:
            lag2 = jnp.maximum(t - lane2, 0).astype(F32)
            msk2 = (lane2 <= t).astype(F32)
        pm = jnp.exp(lam_col * lag2) * msk2
        pt_re = pm * jnp.cos(ang_col * lag2)
        pt_im = pm * jnp.sin(ang_col * lag2)
        cb_re = jnp.concatenate(
            [c_re_ref[0, d, pl.ds(p, 1), :] * bbt_re - c_im_ref[0, d, pl.ds(p, 1), :] * bbt_im
             for p in range(p_dim)], axis=0)
        cb_im = jnp.concatenate(
            [c_re_ref[0, d, pl.ds(p, 1), :] * bbt_im + c_im_ref[0, d, pl.ds(p, 1), :] * bbt_re
             for p in range(p_dim)], axis=0)
        kv = kv + (jnp.dot(cb_re, pt_re, preferred_element_type=F32, precision=HIGHEST)
                   - jnp.dot(cb_im, pt_im, preferred_element_type=F32, precision=HIGHEST))

        lag_col = (t - 1.0 - sub1) if d == 0 else sub1
        pw_m = jnp.exp(lag_col * lam_row)
        pw_re = pw_m * jnp.cos(lag_col * ang_row)
        pw_im = pw_m * jnp.sin(lag_col * ang_row)
        for q in range(p_dim):
            br = bbt_re[q:q + 1, :]
            bi = bbt_im[q:q + 1, :]
            bs_ref[0, q * t:(q + 1) * t, (2 * d) * LANES:(2 * d) * LANES + n] = \
                (pw_re * br - pw_im * bi).astype(bs_ref.dtype)
            bs_ref[0, q * t:(q + 1) * t, (2 * d + 1) * LANES:(2 * d + 1) * LANES + n] = \
                (pw_re * bi + pw_im * br).astype(bs_ref.dtype)

        lag_row = (lane1 + 1.0) if d == 0 else (t - lane1)
        qm = jnp.exp(lam_col * lag_row)
        q_re = qm * jnp.cos(ang_col * lag_row)
        q_im = qm * jnp.sin(ang_col * lag_row)
        for p in range(p_dim):
            cr = ct_re_ref[0, d, :, p:p + 1]
            ci = ct_im_ref[0, d, :, p:p + 1]
            wst_ref[0, (2 * d) * LANES:(2 * d) * LANES + n, p * t:(p + 1) * t] = \
                (cr * q_re - ci * q_im).astype(wst_ref.dtype)
            wst_ref[0, (2 * d + 1) * LANES:(2 * d + 1) * LANES + n, p * t:(p + 1) * t] = \
                (-(cr * q_im + ci * q_re)).astype(wst_ref.dtype)

        tt = float(t)
        at_m = jnp.exp(lam_row * tt)
        coef_ref[0, 2 * d:2 * d + 1, 0:n] = at_m * jnp.cos(ang_row * tt)
        coef_ref[0, 2 * d + 1:2 * d + 2, 0:n] = at_m * jnp.sin(ang_row * tt)

    kv_ref[...] = kv

    def toep_block(p, carry):
        kv_p = kv_ref[pl.ds(pl.multiple_of(p * p_dim, p_dim), p_dim), :]
        for q in range(p_dim):
            row = jnp.broadcast_to(kv_p[q:q + 1, :], (t, 2 * t))
            rolled = pltpu.roll(row, 0, axis=1, stride=1, stride_axis=0)
            toep_ref[0, q * t:(q + 1) * t, pl.ds(pl.multiple_of(p * t, t), t)] = \
                rolled[:, t:].astype(toep_ref.dtype)
        return carry

    lax.fori_loop(0, p_dim, toep_block, 0)


def _s5_prep_call(a_re, a_im, log_step, b_re, b_im, c_re, c_im, d_skip):
    g_dim = a_re.shape[1]
    n, p_dim, t = SSM_N, SSM_P, T_CHUNK
    gm = lambda v: jnp.swapaxes(v, 0, 1)
    ar, ai = gm(a_re), gm(a_im)
    dpq = (jnp.eye(p_dim, dtype=F32)[None] * d_skip.reshape(g_dim, p_dim, 1)).reshape(g_dim, p_dim * p_dim, 1)
    args = (ar[:, :, None, :], ai[:, :, None, :], ar[:, :, :, None], ai[:, :, :, None],
            gm(log_step)[:, :, None, None],
            jnp.swapaxes(gm(b_re), 2, 3), jnp.swapaxes(gm(b_im), 2, 3),
            gm(c_re), gm(c_im), jnp.swapaxes(gm(c_re), 2, 3), jnp.swapaxes(gm(c_im), 2, 3), dpq)

    def spec(v):
        blk = (1,) + v.shape[1:]
        nd = v.ndim
        return pl.BlockSpec(blk, lambda g: (g,) + (0,) * (nd - 1))

    out_shapes = [jax.ShapeDtypeStruct((g_dim, p_dim * t, p_dim * t), BF16),
                  jax.ShapeDtypeStruct((g_dim, p_dim * t, 4 * LANES), BF16),
                  jax.ShapeDtypeStruct((g_dim, 4 * LANES, p_dim * t), BF16),
                  jax.ShapeDtypeStruct((g_dim, SUBLANES, LANES), F32)]
    return pl.pallas_call(
        _s5_prep_body,
        grid=(g_dim,),
        in_specs=[spec(v) for v in args],
        out_specs=[spec(s) for s in out_shapes],
        out_shape=out_shapes,
        scratch_shapes=[pltpu.VMEM((p_dim * p_dim, 2 * t), F32)],
        compiler_params=_cparams(1),
        name="s5_prep",
    )(*args)


def _s5_body(*refs, nb, nc, with_y):
    if with_y:
        u_ref, bs_ref, coef_ref, h0_ref, toep_ref, wst_ref, y_ref, fin_ref, s_scr, h_scr = refs
    else:
        u_ref, bs_ref, coef_ref, h0_ref, fin_ref, s_scr = refs
    p_dim = u_ref.shape[1]
    t = T_CHUNK
    u = jnp.concatenate([u_ref[0, q] for q in range(p_dim)], axis=1).astype(BF16)
    s_scr[...] = jnp.dot(u, bs_ref[0], preferred_element_type=F32)
    if with_y:
        h_scr[...] = jnp.zeros(h_scr.shape, h_scr.dtype)

    af_re, af_im = coef_ref[0, 0:1, :], coef_ref[0, 1:2, :]
    ab_re, ab_im = coef_ref[0, 2:3, :], coef_ref[0, 3:4, :]
    init = tuple(h0_ref[0, r:r + 1, :] for r in range(4 * nb))

    blk = SUBLANES if nc % SUBLANES == 0 else nc

    def block(k, carry):
        new = list(carry)
        for b in range(nb):
            f0 = b * nc + blk * k
            b0 = b * nc + nc - blk - blk * k
            if blk == SUBLANES:
                f0 = pl.multiple_of(f0, SUBLANES)
                b0 = pl.multiple_of(b0, SUBLANES)
            sf_re = s_scr[pl.ds(f0, blk), 0:LANES]
            sf_im = s_scr[pl.ds(f0, blk), LANES:2 * LANES]
            sb_re = s_scr[pl.ds(b0, blk), 2 * LANES:3 * LANES]
            sb_im = s_scr[pl.ds(b0, blk), 3 * LANES:4 * LANES]
            fr, fi, br, bi = new[4 * b:4 * b + 4]
            hf_re, hf_im, hb_re, hb_im = [], [], [], []
            for j in range(blk):
                jb = blk - 1 - j
                hf_re.append(fr)
                hf_im.append(fi)
                hb_re.insert(0, br)
                hb_im.insert(0, bi)
                fr, fi = (af_re * fr - af_im * fi + sf_re[j:j + 1],
                          af_re * fi + af_im * fr + sf_im[j:j + 1])
                br, bi = (ab_re * br - ab_im * bi + sb_re[jb:jb + 1],
                          ab_re * bi + ab_im * br + sb_im[jb:jb + 1])
            if with_y:
                h_scr[pl.ds(f0, blk), 0:LANES] = jnp.concatenate(hf_re, axis=0)
                h_scr[pl.ds(f0, blk), LANES:2 * LANES] = jnp.concatenate(hf_im, axis=0)
                h_scr[pl.ds(b0, blk), 2 * LANES:3 * LANES] = jnp.concatenate(hb_re, axis=0)
                h_scr[pl.ds(b0, blk), 3 * LANES:4 * LANES] = jnp.concatenate(hb_im, axis=0)
            new[4 * b:4 * b + 4] = [fr, fi, br, bi]
        return tuple(new)

    if nc == blk:
        fin = block(0, init)
    else:
        fin = lax.fori_loop(0, nc // blk, block, init)
    fin_ref[0] = jnp.concatenate(list(fin) + [jnp.zeros((1, LANES), F32)] * (SUBLANES - 4 * nb), axis=0) \
        if 4 * nb < SUBLANES else jnp.concatenate(list(fin), axis=0)

    if with_y:
        y = jnp.dot(u, toep_ref[0], preferred_element_type=F32)
        y = y + jnp.dot(h_scr[...].astype(BF16), wst_ref[0], preferred_element_type=F32)
        y = _gelu(y)
        for p in range(p_dim):
            y_ref[0, p] = y[:, p * t:(p + 1) * t]


def _s5_call(u4, bs, coef, h0, toep=None, wst=None, *, nb, nc):
    g_dim, p_dim, rows, t = u4.shape
    with_y = toep is not None
    body = functools.partial(_s5_body, nb=nb, nc=nc, with_y=with_y)

    def spec(v):
        blk = (1,) + v.shape[1:]
        nd = v.ndim
        return pl.BlockSpec(blk, lambda g: (g,) + (0,) * (nd - 1))

    fin_shape = jax.ShapeDtypeStruct((g_dim, SUBLANES, LANES), F32)
    args = [u4, bs, coef, h0]
    scratch = [pltpu.VMEM((rows, 4 * LANES), F32)]
    if with_y:
        args += [toep, wst]
        out_shape = [jax.ShapeDtypeStruct(u4.shape, F32), fin_shape]
        scratch.append(pltpu.VMEM((rows, 4 * LANES), F32))
    else:
        out_shape = [fin_shape]
    outs = pl.pallas_call(
        body,
        grid=(g_dim,),
        in_specs=[spec(v) for v in args],
        out_specs=[spec(s) for s in out_shape],
        out_shape=out_shape,
        scratch_shapes=scratch,
        compiler_params=_cparams(1),
        name="s5_chunks" if with_y else "s5_ctx_states",
    )(*args)
    return outs


def _even_out_body(yc_ref, yt_ref, gwt_ref, gb_ref, woa_ref, wob_ref, x_ref, g1_ref, o_ref):
    yt = yt_ref[...]
    gl = jnp.dot(gwt_ref[...], yt.astype(BF16), preferred_element_type=F32) + gb_ref[...]
    ys = (yt * _sigmoid(gl)).astype(BF16)
    mix = jnp.dot(yc_ref[...], woa_ref[...], preferred_element_type=F32)
    mix = mix + jnp.dot(ys.T, wob_ref[...], preferred_element_type=F32)
    o_ref[...] = x_ref[...] + g1_ref[0] * mix


def _even_out_call(yc, yt, gwt, gb, woa, wob, x2d, g1, seq_len, tm):
    tokens, d = x2d.shape
    cw = yc.shape[1]
    sw = yt.shape[0]
    tiles_per_seq = seq_len // tm
    return pl.pallas_call(
        _even_out_body,
        grid=(tokens // tm,),
        in_specs=[pl.BlockSpec((tm, cw), lambda i: (i, 0)),
                  pl.BlockSpec((sw, tm), lambda i: (0, i)),
                  _resident((sw, sw)), _resident((sw, 1)),
                  _resident((cw, d)), _resident((sw, d)),
                  pl.BlockSpec((tm, d), lambda i: (i, 0)),
                  pl.BlockSpec((1, 1, d), lambda i: (i // tiles_per_seq, 0, 0))],
        out_specs=pl.BlockSpec((tm, d), lambda i: (i, 0)),
        out_shape=jax.ShapeDtypeStruct((tokens, d), F32),
        compiler_params=_cparams(1),
        name="even_out_proj",
    )(yc, yt, gwt, gb, woa, wob, x2d, g1)


def _odd_body(x_ref, g_ref, sh_ref, sc_ref, wu_ref, wv_ref, lg_ref, lb_ref, sw_ref, sb_ref,
              wo_ref, g1_ref, o_ref, gated_ref):
    x = x_ref[...]
    tm = x.shape[0]
    hn = _norm_mod(x, g_ref[...], sh_ref[0], sc_ref[0]).astype(BF16)
    v = _gelu(jnp.dot(hn, wv_ref[...], preferred_element_type=F32))
    v = _layer_norm(v, lg_ref[...], lb_ref[...]).astype(BF16)
    u = _gelu(jnp.dot(hn, wu_ref[...], preferred_element_type=F32))
    hd = v.shape[1] // SGU_HEADS
    for ch in range(tm // SGU_CHUNK):
        r0 = ch * SGU_CHUNK
        for h in range(SGU_HEADS):
            s = jnp.dot(sw_ref[h], v[r0:r0 + SGU_CHUNK, h * hd:(h + 1) * hd],
                        preferred_element_type=F32) + sb_ref[h]
            gated_ref[r0:r0 + SGU_CHUNK, h * hd:(h + 1) * hd] = \
                (u[r0:r0 + SGU_CHUNK, h * hd:(h + 1) * hd] * s).astype(BF16)
    mix = jnp.dot(gated_ref[...], wo_ref[...], preferred_element_type=F32)
    o_ref[...] = x + g1_ref[0] * mix


def _odd_call(x2d, g, shift, scale, wu, wv, ln_g, ln_b, sgu_w, sgu_b, wo, g1, seq_len, tm):
    tokens, d = x2d.shape
    w = wu.shape[1]
    tiles_per_seq = seq_len // tm
    mod_spec = pl.BlockSpec((1, 1, d), lambda i: (i // tiles_per_seq, 0, 0))
    return pl.pallas_call(
        _odd_body,
        grid=(tokens // tm,),
        in_specs=[pl.BlockSpec((tm, d), lambda i: (i, 0)),
                  _resident((1, d)), mod_spec, mod_spec,
                  _resident((d, w)), _resident((d, w)), _resident((1, w)), _resident((1, w)),
                  _resident(sgu_w.shape), _resident(sgu_b.shape),
                  _resident((w, d)), mod_spec],
        out_specs=pl.BlockSpec((tm, d), lambda i: (i, 0)),
        out_shape=jax.ShapeDtypeStruct((tokens, d), F32),
        scratch_shapes=[pltpu.VMEM((tm, w), BF16)],
        compiler_params=_cparams(1),
        name="odd_gmlp",
    )(x2d, g, shift, scale, wu, wv, ln_g, ln_b, sgu_w, sgu_b, wo, g1)


def _ffn_up_body(x_ref, g_ref, sh_ref, sc_ref, w_ref, z_ref, hn_ref):
    @pl.when(pl.program_id(1) == 0)
    def _():
        hn_ref[...] = _norm_mod(x_ref[...], g_ref[...], sh_ref[0], sc_ref[0]).astype(BF16)

    z_ref[...] = jnp.dot(hn_ref[...], w_ref[...], preferred_element_type=F32).astype(z_ref.dtype)


def _ffn_up_call(x2d, g, shift, scale, w, seq_len, tm, tn):
    tokens, d = x2d.shape
    n = w.shape[1]
    tiles_per_seq = seq_len // tm
    mod_spec = pl.BlockSpec((1, 1, d), lambda i, j: (i // tiles_per_seq, 0, 0))
    return pl.pallas_call(
        _ffn_up_body,
        grid=(tokens // tm, n // tn),
        in_specs=[pl.BlockSpec((tm, d), lambda i, j: (i, 0)),
                  _resident((1, d)), mod_spec, mod_spec,
                  pl.BlockSpec((d, tn), lambda i, j: (0, j))],
        out_specs=pl.BlockSpec((tm, tn), lambda i, j: (i, j)),
        out_shape=jax.ShapeDtypeStruct((tokens, n), BF16),
        scratch_shapes=[pltpu.VMEM((tm, d), BF16)],
        compiler_params=_cparams(2),
        name="ffn_up_proj",
    )(x2d, g, shift, scale, w)


FFN_PIECE = 256


def _ffn_down_body(zp_ref, zm_ref, zn_ref, cw_ref, cb_ref, wo_ref, x_ref, g2_ref, fg_ref, o_ref,
                   act0_ref, act1_ref, acc_ref, *, tm, tiles_per_seq, n_tiles, final_norm,
                   k_slices, anchor_dist):
    i = pl.program_id(0)
    it = jnp.minimum(i, n_tiles - 1)
    first = (it % tiles_per_seq) == 0
    last = (it % tiles_per_seq) == tiles_per_seq - 1
    dff = act0_ref.shape[1]
    d = o_ref.shape[1]
    n_rb = tm // GRID_W
    n_pieces = d // FFN_PIECE
    cb_per_piece = dff // LANES // n_pieces
    n_vr = GRID_W // SUBLANES
    sub_id = lax.broadcasted_iota(jnp.int32, (SUBLANES, LANES), 0)
    sub_first = sub_id == 0
    sub_last = sub_id == SUBLANES - 1
    top_ok = jnp.where(first, 0.0, 1.0)
    bot_ok = jnp.where(last, 0.0, 1.0)

    def from_left(v):
        r = [pltpu.roll(v[SUBLANES * k:SUBLANES * (k + 1)], 1, axis=0) for k in range(n_vr)]
        return jnp.concatenate([jnp.where(sub_first, r[k - 1] if k else 0.0, r[k])
                                for k in range(n_vr)], axis=0)

    def from_right(v):
        r = [pltpu.roll(v[SUBLANES * k:SUBLANES * (k + 1)], SUBLANES - 1, axis=0) for k in range(n_vr)]
        return jnp.concatenate([jnp.where(sub_last, r[k + 1] if k < n_vr - 1 else 0.0, r[k])
                                for k in range(n_vr)], axis=0)

    def grid_row(rb, c0):
        if rb < 0:
            return zp_ref[:, pl.ds(c0, LANES)].astype(F32) * top_ok
        if rb >= n_rb:
            return zn_ref[:, pl.ds(c0, LANES)].astype(F32) * bot_ok
        return zm_ref[rb * GRID_W:(rb + 1) * GRID_W, pl.ds(c0, LANES)].astype(F32)

    def conv_rows(rb, c0, dep):
        up, mid, dn = grid_row(rb - 1, c0), grid_row(rb, c0), grid_row(rb + 1, c0)
        w = lambda k: cw_ref[pl.ds(k, 1), pl.ds(c0, LANES)]
        left = up * w(0) + mid * w(3) + dn * w(6)
        cent = up * w(1) + mid * w(4) + dn * w(7)
        right = up * w(2) + mid * w(5) + dn * w(8)
        return cent + from_left(left) + from_right(right) + (cb_ref[:, pl.ds(c0, LANES)] + dep)

    @pl.when(i == 0)
    def _():
        act1_ref[...] = jnp.zeros(act1_ref.shape, act1_ref.dtype)

    def run(act_w, act_r):
        def piece(n, carry):
            cn = pl.multiple_of(n * FFN_PIECE, FFN_PIECE)
            n_items = cb_per_piece * n_rb
            per_slice = n_items // k_slices
            kc = dff // k_slices
            deps = list(carry)
            for j in range(cb_per_piece):
                ca = pl.multiple_of((n * cb_per_piece + j) * LANES, LANES)
                cg = pl.multiple_of(dff + (n * cb_per_piece + j) * LANES, LANES)
                for rb in range(n_rb):
                    idx = j * n_rb + rb
                    dep = deps[idx // per_slice]
                    a = conv_rows(rb, ca, dep)
                    g = conv_rows(rb, cg, dep)
                    act_w[rb * GRID_W:(rb + 1) * GRID_W, pl.ds(ca, LANES)] = (a * _silu(g)).astype(BF16)
                    if (idx + 1) % per_slice:
                        continue
                    m = idx // per_slice
                    part = jnp.dot(act_r[:, m * kc:(m + 1) * kc],
                                   wo_ref[m * kc:(m + 1) * kc, pl.ds(cn, FFN_PIECE)],
                                   preferred_element_type=F32)
                    if m == 0:
                        acc_ref[...] = part
                    elif m < k_slices - 1:
                        acc_ref[...] += part
                    else:
                        o_ref[:, pl.ds(cn, FFN_PIECE)] = x_ref[:, pl.ds(cn, FFN_PIECE)] + \
                            g2_ref[0, :, pl.ds(cn, FFN_PIECE)] * (acc_ref[...] + part)
                    deps.append(part[0:1, 0:LANES] * 0.0)
            return tuple(deps[-anchor_dist:])

        zero_row = jnp.zeros((1, LANES), F32)
        lax.fori_loop(0, n_pieces, piece, (zero_row,) * anchor_dist)

    @pl.when(i % 2 == 0)
    def _():
        run(act0_ref, act1_ref)

    @pl.when(i % 2 == 1)
    def _():
        run(act1_ref, act0_ref)

    if final_norm:
        y = o_ref[...]
        ms = jnp.mean(y * y, axis=-1, keepdims=True)
        o_ref[...] = y * lax.rsqrt(ms + RMS_EPS) * fg_ref[...]


def _ffn_down_call(z, conv_w9, conv_b, wo, x2d, g2, final_g, seq_len, tm, final_norm, k_slices, anchor_dist):
    tokens, d = x2d.shape
    n2 = z.shape[1]
    dff = n2 // 2
    tiles_per_seq = seq_len // tm
    n_tiles = tokens // tm
    hb = tm // GRID_W
    n_hblocks = tokens // GRID_W
    cw_pad = jnp.pad(conv_w9, ((0, 16 - 9), (0, 0)))
    body = functools.partial(_ffn_down_body, tm=tm, tiles_per_seq=tiles_per_seq, n_tiles=n_tiles,
                             final_norm=final_norm, k_slices=k_slices, anchor_dist=anchor_dist)
    conv_tile = lambda i: jnp.minimum(i, n_tiles - 1)
    out_tile = lambda i: jnp.maximum(i - 1, 0)
    return pl.pallas_call(
        body,
        grid=(n_tiles + 1,),
        in_specs=[pl.BlockSpec((GRID_W, n2), lambda i: (jnp.maximum(conv_tile(i) * hb - 1, 0), 0)),
                  pl.BlockSpec((tm, n2), lambda i: (conv_tile(i), 0)),
                  pl.BlockSpec((GRID_W, n2),
                               lambda i: (jnp.minimum((conv_tile(i) + 1) * hb, n_hblocks - 1), 0)),
                  _resident((16, n2)), _resident((1, n2)), _resident((dff, d)),
                  pl.BlockSpec((tm, d), lambda i: (out_tile(i), 0)),
                  pl.BlockSpec((1, 1, d), lambda i: (out_tile(i) // tiles_per_seq, 0, 0)),
                  _resident((1, d))],
        out_specs=pl.BlockSpec((tm, d), lambda i: (out_tile(i), 0)),
        out_shape=jax.ShapeDtypeStruct((tokens, d), F32),
        scratch_shapes=[pltpu.VMEM((tm, dff), BF16), pltpu.VMEM((tm, dff), BF16),
                        pltpu.VMEM((tm, FFN_PIECE), F32)],
        compiler_params=_cparams(1),
        name="ffn_conv_down",
    )(z, z, z, cw_pad, conv_b.reshape(1, n2), wo, x2d, g2, final_g.reshape(1, d))


def _pick_tile(seq_len, want):
    tm = min(want, seq_len)
    assert seq_len % tm == 0
    return tm


def kernel(x, c, ctx, c_ctx, ada_w, ada_b, norm_g, final_g, e_w_in, e_conv_w, e_conv_b, e_ln_g, e_ln_b, s5_a_re, s5_a_im, s5_log_step, s5_b_re, s5_b_im, s5_c_re, s5_c_im, s5_d, s5_glu_w, s5_glu_b, e_w_out, o_w_in, o_ln_g, o_ln_b, o_sgu_w, o_sgu_b, o_w_out, f_w_in, f_conv_w, f_conv_b, f_w_out):
    bn, seq_len, d = x.shape
    ctx_len = ctx.shape[1]
    depth = ada_w.shape[0]
    assert depth == 2 and bn == 2, "block structure is specialised to depth 2, batch 2"
    tokens = bn * seq_len
    cw = e_conv_w.shape[2]
    sw = s5_d.shape[1]
    g_dim = sw // SSM_P
    t = T_CHUNK
    nc = seq_len // t
    nc_ctx = ctx_len // t
    assert seq_len % t == 0 and ctx_len % t == 0 and seq_len % GRID_W == 0

    x2d = x.reshape(tokens, d)

    s_rows = jnp.zeros((SUBLANES, d), F32).at[:bn].set(c).at[bn].set(c_ctx)
    mods = _ada_call(s_rows, ada_w, ada_b)

    def lat_mod(layer, k):
        return mods[layer, :bn, k * d:(k + 1) * d].reshape(bn, 1, d)

    li = 0
    w_in = e_w_in[li]
    wv = w_in[:, :cw].astype(BF16)
    wg = w_in[:, cw:2 * cw].astype(BF16)
    wut = w_in[:, 2 * cw:].T.astype(BF16)
    toep, bs, wst, coef = _s5_prep_call(s5_a_re[li], s5_a_im[li], s5_log_step[li], s5_b_re[li],
                                        s5_b_im[li], s5_c_re[li], s5_c_im[li], s5_d[li])

    g_row = norm_g[0, 0].reshape(1, d)
    ctx_shift = mods[0, bn:bn + 1, 0:d]
    ctx_scale = mods[0, bn:bn + 1, d:2 * d]
    uct = _ctx_in_call(ctx.reshape(bn * ctx_len, d), g_row, ctx_shift, ctx_scale, wut)
    uc4 = uct.reshape(g_dim, SSM_P, bn * nc_ctx, t)
    rows_ctx = -(-(bn * nc_ctx) // SUBLANES) * SUBLANES
    uc4 = jnp.pad(uc4, ((0, 0), (0, 0), (0, rows_ctx - bn * nc_ctx), (0, 0)))
    zero_h0 = jnp.zeros((g_dim, SUBLANES, LANES), F32)
    (h0s,) = _s5_call(uc4, bs, coef, zero_h0, nb=bn, nc=nc_ctx)

    tm_in = _pick_tile(seq_len, 512)
    z, ut = _even_in_call(x2d, g_row, lat_mod(0, 0), lat_mod(0, 1), wv, wg, wut, seq_len, tm_in)
    yc = _conv_call(z, e_conv_w[li], e_conv_b[li], e_ln_g[li], e_ln_b[li], seq_len,
                    _pick_tile(seq_len, 256))
    u4 = ut.reshape(g_dim, SSM_P, bn * nc, t)
    y4, _ = _s5_call(u4, bs, coef, h0s, toep, wst, nb=bn, nc=nc)
    yt = y4.reshape(sw, tokens)
    w_out = e_w_out[li]
    h = _even_out_call(yc, yt, s5_glu_w[li].T.astype(BF16), s5_glu_b[li].reshape(sw, 1),
                       w_out[:cw].astype(BF16), w_out[cw:].astype(BF16), x2d, lat_mod(0, 2),
                       seq_len, _pick_tile(seq_len, 512))

    def conv_ffn(h, layer, final_norm):
        zf = _ffn_up_call(h, norm_g[layer, 1].reshape(1, d), lat_mod(layer, 3), lat_mod(layer, 4),
                          f_w_in[layer].astype(BF16), seq_len,
                          _pick_tile(seq_len, 1024 if layer == 0 else 512), 1024 if layer == 0 else 2048)
        return _ffn_down_call(zf, f_conv_w[layer].reshape(9, -1), f_conv_b[layer],
                              f_w_out[layer].astype(BF16), h, lat_mod(layer, 5), final_g,
                              seq_len, _pick_tile(seq_len, 256), final_norm,
                              k_slices=8 if layer == 0 else 16, anchor_dist=2 if layer == 0 else 4)

    h = conv_ffn(h, 0, False)

    li = 0
    w = o_w_in.shape[2] // 2
    h = _odd_call(h, norm_g[1, 0].reshape(1, d), lat_mod(1, 0), lat_mod(1, 1),
                  o_w_in[li][:, :w].astype(BF16), o_w_in[li][:, w:].astype(BF16),
                  o_ln_g[li].reshape(1, w), o_ln_b[li].reshape(1, w),
                  o_sgu_w[li].astype(BF16), o_sgu_b[li].reshape(SGU_HEADS, SGU_CHUNK, 1),
                  o_w_out[li].astype(BF16), lat_mod(1, 2), seq_len, _pick_tile(seq_len, 256))
    h = conv_ffn(h, 1, True)
    return h.reshape(bn, seq_len, d)
```

```python
import functools
import math

import jax
import jax.numpy as jnp
from jax import lax
from jax.experimental import pallas as pl
from jax.experimental.pallas import tpu as pltpu

F32 = jnp.float32
BF16 = jnp.bfloat16
HIGHEST = lax.Precision.HIGHEST

RMS_EPS = 1e-6
LN_EPS = 1e-5
GRID_W = 64
CONV_K = 31
CONV_HALO = 16
SSM_P = 16
SSM_N = 64
SGU_CHUNK = 128
SGU_HEADS = 8
T_CHUNK = 128
LANES = 128
SUBLANES = 8
VMEM_LIMIT_BYTES = 56 * 1024 * 1024


def _cparams(ngrid):
    return pltpu.CompilerParams(dimension_semantics=("arbitrary",) * ngrid,
                                vmem_limit_bytes=VMEM_LIMIT_BYTES)


def _resident(shape):
    nd = len(shape)
    return pl.BlockSpec(shape, lambda *_: (0,) * nd, pipeline_mode=pl.Buffered(1))


def _sigmoid(x):
    return 1.0 / (1.0 + jnp.exp(-x))


def _silu(x):
    return x * _sigmoid(x)


def _gelu(x):
    return 0.5 * x * (1.0 + jnp.tanh(math.sqrt(2.0 / math.pi) * (x + 0.044715 * (x * x * x))))


def _norm_mod(x, g, shift, scale):
    ms = jnp.mean(x * x, axis=-1, keepdims=True)
    y = x * lax.rsqrt(ms + RMS_EPS) * g
    return y * (1.0 + scale) + shift


def _layer_norm(x, g, b):
    mu = jnp.mean(x, axis=-1, keepdims=True)
    xc = x - mu
    var = jnp.mean(xc * xc, axis=-1, keepdims=True)
    return xc * lax.rsqrt(var + LN_EPS) * g + b


def _ada_body(s_ref, w_ref, b_ref, o_ref):
    s = _silu(s_ref[...])
    o_ref[0] = jnp.dot(s, w_ref[0], preferred_element_type=F32, precision=HIGHEST) + b_ref[0]


def _ada_call(s_rows, ada_w, ada_b):
    depth, d, n = ada_w.shape
    tn = 1024
    return pl.pallas_call(
        _ada_body,
        grid=(depth, n // tn),
        in_specs=[pl.BlockSpec((SUBLANES, d), lambda l, j: (0, 0)),
                  pl.BlockSpec((1, d, tn), lambda l, j: (l, 0, j)),
                  pl.BlockSpec((1, 1, tn), lambda l, j: (l, 0, j))],
        out_specs=pl.BlockSpec((1, SUBLANES, tn), lambda l, j: (l, 0, j)),
        out_shape=jax.ShapeDtypeStruct((depth, SUBLANES, n), F32),
        compiler_params=_cparams(2),
        name="ada_mod",
    )(s_rows, ada_w, ada_b.reshape(depth, 1, n))


def _even_in_body(x_ref, g_ref, sh_ref, sc_ref, wv_ref, wg_ref, wut_ref, z_ref, ut_ref):
    hn = _norm_mod(x_ref[...], g_ref[...], sh_ref[0], sc_ref[0]).astype(BF16)
    av = jnp.dot(hn, wv_ref[...], preferred_element_type=F32)
    ag = jnp.dot(hn, wg_ref[...], preferred_element_type=F32)
    z_ref[...] = av * _sigmoid(ag)
    ut_ref[...] = lax.dot_general(wut_ref[...], hn, (((1,), (1,)), ((), ())),
                                  preferred_element_type=F32)


def _even_in_call(x2d, g, shift, scale, wv, wg, wut, seq_len, tm):
    tokens, d = x2d.shape
    cw = wv.shape[1]
    sw = wut.shape[0]
    tiles_per_seq = seq_len // tm
    mod_spec = pl.BlockSpec((1, 1, d), lambda i: (i // tiles_per_seq, 0, 0))
    return pl.pallas_call(
        _even_in_body,
        grid=(tokens // tm,),
        in_specs=[pl.BlockSpec((tm, d), lambda i: (i, 0)),
                  _resident((1, d)), mod_spec, mod_spec,
                  _resident((d, cw)), _resident((d, cw)), _resident((sw, d))],
        out_specs=[pl.BlockSpec((tm, cw), lambda i: (i, 0)),
                   pl.BlockSpec((sw, tm), lambda i: (0, i))],
        out_shape=[jax.ShapeDtypeStruct((tokens, cw), F32),
                   jax.ShapeDtypeStruct((sw, tokens), F32)],
        compiler_params=_cparams(1),
        name="even_in_proj",
    )(x2d, g, shift, scale, wv, wg, wut)


def _ctx_in_body(x_ref, g_ref, sh_ref, sc_ref, wut_ref, ut_ref):
    hn = _norm_mod(x_ref[...], g_ref[...], sh_ref[...], sc_ref[...]).astype(BF16)
    ut_ref[...] = lax.dot_general(wut_ref[...], hn, (((1,), (1,)), ((), ())),
                                  preferred_element_type=F32)


def _ctx_in_call(c2d, g, shift, scale, wut):
    tokens, d = c2d.shape
    sw = wut.shape[0]
    return pl.pallas_call(
        _ctx_in_body,
        out_shape=jax.ShapeDtypeStruct((sw, tokens), F32),
        compiler_params=pltpu.CompilerParams(vmem_limit_bytes=VMEM_LIMIT_BYTES),
        name="ctx_in_proj",
    )(c2d, g, shift, scale, wut)


CONV_ROWS = 128


def _conv_body(zp_ref, zm_ref, zn_ref, w_ref, b_ref, g_ref, beta_ref, o_ref, buf_ref, acc_ref,
               *, tl, tiles_per_seq):
    i = pl.program_id(0)
    first = (i % tiles_per_seq) == 0
    last = (i % tiles_per_seq) == tiles_per_seq - 1
    buf_ref[0:CONV_HALO, :] = jnp.where(first, 0.0, zp_ref[...])
    buf_ref[CONV_HALO:CONV_HALO + tl, :] = zm_ref[...]
    buf_ref[CONV_HALO + tl:2 * CONV_HALO + tl, :] = jnp.where(last, 0.0, zn_ref[...])

    cw = o_ref.shape[1]
    n_rb = tl // CONV_ROWS
    n_cb = cw // LANES
    rows = CONV_ROWS + SUBLANES

    def lane_block(r0, c0):
        acc = jnp.zeros((CONV_ROWS, LANES), F32)
        for r in range(SUBLANES):
            part = None
            for a in range(4):
                k = SUBLANES * a + r - 1
                if k < 0 or k >= CONV_K:
                    continue
                term = buf_ref[pl.ds(r0 + SUBLANES * a, rows), c0:c0 + LANES] * \
                    w_ref[k:k + 1, c0:c0 + LANES]
                part = term if part is None else part + term
            if r:
                part = pltpu.roll(part, rows - r, axis=0)
            acc = acc + part[:CONV_ROWS]
        acc_ref[pl.ds(r0, CONV_ROWS), c0:c0 + LANES] = acc + b_ref[:, c0:c0 + LANES]

    def block(rb, carry):
        r0 = pl.multiple_of(rb * CONV_ROWS, CONV_ROWS)
        for cb in range(n_cb):
            lane_block(r0, cb * LANES)
        return carry

    lax.fori_loop(0, n_rb, block, 0)

    ln_rows = 128

    def ln_block(rb, carry):
        r0 = pl.multiple_of(rb * ln_rows, ln_rows)
        y = _layer_norm(acc_ref[pl.ds(r0, ln_rows), :], g_ref[...], beta_ref[...])
        o_ref[pl.ds(r0, ln_rows), :] = _silu(y).astype(o_ref.dtype)
        return carry

    lax.fori_loop(0, tl // ln_rows, ln_block, 0)


def _conv_call(z2d, conv_w, conv_b, ln_g, ln_b, seq_len, tl):
    tokens, cw = z2d.shape
    tiles_per_seq = seq_len // tl
    hb = tl // CONV_HALO
    n_hblocks = tokens // CONV_HALO
    w_pad = jnp.pad(conv_w, ((0, 32 - CONV_K), (0, 0)))
    body = functools.partial(_conv_body, tl=tl, tiles_per_seq=tiles_per_seq)
    return pl.pallas_call(
        body,
        grid=(tokens // tl,),
        in_specs=[pl.BlockSpec((CONV_HALO, cw), lambda i: (jnp.maximum(i * hb - 1, 0), 0)),
                  pl.BlockSpec((tl, cw), lambda i: (i, 0)),
                  pl.BlockSpec((CONV_HALO, cw), lambda i: (jnp.minimum((i + 1) * hb, n_hblocks - 1), 0)),
                  _resident((32, cw)), _resident((1, cw)), _resident((1, cw)), _resident((1, cw))],
        out_specs=pl.BlockSpec((tl, cw), lambda i: (i, 0)),
        out_shape=jax.ShapeDtypeStruct((tokens, cw), BF16),
        scratch_shapes=[pltpu.VMEM((tl + 2 * CONV_HALO, cw), F32), pltpu.VMEM((tl, cw), F32)],
        compiler_params=_cparams(1),
        name="conformer_conv",
    )(z2d, z2d, z2d, w_pad, conv_b.reshape(1, cw), ln_g.reshape(1, cw), ln_b.reshape(1, cw))


def _s5_prep_body(ar_row_ref, ai_row_ref, ar_col_ref, ai_col_ref, ls_ref,
                  bt_re_ref, bt_im_ref, c_re_ref, c_im_ref, ct_re_ref, ct_im_ref, dpq_ref,
                  toep_ref, bs_ref, wst_ref, coef_ref, kv_ref):
    t = T_CHUNK
    n = SSM_N
    p_dim = SSM_P
    lane2 = lax.broadcasted_iota(jnp.int32, (1, 2 * t), 1)
    lane1 = lax.broadcasted_iota(jnp.int32, (1, t), 1).astype(F32)
    sub1 = lax.broadcasted_iota(jnp.int32, (t, 1), 0).astype(F32)

    bs_ref[...] = jnp.zeros(bs_ref.shape, bs_ref.dtype)
    wst_ref[...] = jnp.zeros(wst_ref.shape, wst_ref.dtype)
    coef_ref[...] = jnp.zeros(coef_ref.shape, coef_ref.dtype)

    kv = dpq_ref[0] * (lane2 == t).astype(F32)
    for d in range(2):
        dt = jnp.exp(ls_ref[0, d])
        ar = ar_row_ref[0, d]
        ai = ai_row_ref[0, d]
        lam_row = ar * dt
        ang_row = ai * dt
        lam_col = ar_col_ref[0, d] * dt
        ang_col = ai_col_ref[0, d] * dt
        mag = jnp.exp(lam_row)
        ab_re = mag * jnp.cos(ang_row)
        ab_im = mag * jnp.sin(ang_row)
        den = ar * ar + ai * ai
        num_re = ab_re - 1.0
        f_re = (num_re * ar + ab_im * ai) / den
        f_im = (ab_im * ar - num_re * ai) / den
        bbt_re = f_re * bt_re_ref[0, d] - f_im * bt_im_ref[0, d]
        bbt_im = f_re * bt_im_ref[0, d] + f_im * bt_re_ref[0, d]

        if d == 0:
            lag2 = jnp.maximum(lane2 - t, 0).astype(F32)
            msk2 = (lane2 >= t).astype(F32)
        else:
            lag2 = jnp.maximum(t - lane2, 0).astype(F32)
            msk2 = (lane2 <= t).astype(F32)
        pm = jnp.exp(lam_col * lag2) * msk2
        pt_re = pm * jnp.cos(ang_col * lag2)
        pt_im = pm * jnp.sin(ang_col * lag2)
        cb_re = jnp.concatenate(
            [c_re_ref[0, d, pl.ds(p, 1), :] * bbt_re - c_im_ref[0, d, pl.ds(p, 1), :] * bbt_im
             for p in range(p_dim)], axis=0)
        cb_im = jnp.concatenate(
            [c_re_ref[0, d, pl.ds(p, 1), :] * bbt_im + c_im_ref[0, d, pl.ds(p, 1), :] * bbt_re
             for p in range(p_dim)], axis=0)
        kv = kv + (jnp.dot(cb_re, pt_re, preferred_element_type=F32, precision=HIGHEST)
                   - jnp.dot(cb_im, pt_im, preferred_element_type=F32, precision=HIGHEST))

        lag_col = (t - 1.0 - sub1) if d == 0 else sub1
        pw_m = jnp.exp(lag_col * lam_row)
        pw_re = pw_m * jnp.cos(lag_col * ang_row)
        pw_im = pw_m * jnp.sin(lag_col * ang_row)
        for q in range(p_dim):
            br = bbt_re[q:q + 1, :]
            bi = bbt_im[q:q + 1, :]
            bs_ref[0, q * t:(q + 1) * t, (2 * d) * LANES:(2 * d) * LANES + n] = \
                (pw_re * br - pw_im * bi).astype(bs_ref.dtype)
            bs_ref[0, q * t:(q + 1) * t, (2 * d + 1) * LANES:(2 * d + 1) * LANES + n] = \
                (pw_re * bi + pw_im * br).astype(bs_ref.dtype)

        lag_row = (lane1 + 1.0) if d == 0 else (t - lane1)
        qm = jnp.exp(lam_col * lag_row)
        q_re = qm * jnp.cos(ang_col * lag_row)
        q_im = qm * jnp.sin(ang_col * lag_row)
        for p in range(p_dim):
            cr = ct_re_ref[0, d, :, p:p + 1]
            ci = ct_im_ref[0, d, :, p:p + 1]
            wst_ref[0, (2 * d) * LANES:(2 * d) * LANES + n, p * t:(p + 1) * t] = \
                (cr * q_re - ci * q_im).astype(wst_ref.dtype)
            wst_ref[0, (2 * d + 1) * LANES:(2 * d + 1) * LANES + n, p * t:(p + 1) * t] = \
                (-(cr * q_im + ci * q_re)).astype(wst_ref.dtype)

        tt = float(t)
        at_m = jnp.exp(lam_row * tt)
        coef_ref[0, 2 * d:2 * d + 1, 0:n] = at_m * jnp.cos(ang_row * tt)
        coef_ref[0, 2 * d + 1:2 * d + 2, 0:n] = at_m * jnp.sin(ang_row * tt)

    kv_ref[...] = kv

    def toep_block(p, carry):
        kv_p = kv_ref[pl.ds(pl.multiple_of(p * p_dim, p_dim), p_dim), :]
        for q in range(p_dim):
            row = jnp.broadcast_to(kv_p[q:q + 1, :], (t, 2 * t))
            rolled = pltpu.roll(row, 0, axis=1, stride=1, stride_axis=0)
            toep_ref[0, q * t:(q + 1) * t, pl.ds(pl.multiple_of(p * t, t), t)] = \
                rolled[:, t:].astype(toep_ref.dtype)
        return carry

    lax.fori_loop(0, p_dim, toep_block, 0)


def _s5_prep_call(a_re, a_im, log_step, b_re, b_im, c_re, c_im, d_skip):
    g_dim = a_re.shape[1]
    n, p_dim, t = SSM_N, SSM_P, T_CHUNK
    gm = lambda v: jnp.swapaxes(v, 0, 1)
    ar, ai = gm(a_re), gm(a_im)
    dpq = (jnp.eye(p_dim, dtype=F32)[None] * d_skip.reshape(g_dim, p_dim, 1)).reshape(g_dim, p_dim * p_dim, 1)
    args = (ar[:, :, None, :], ai[:, :, None, :], ar[:, :, :, None], ai[:, :, :, None],
            gm(log_step)[:, :, None, None],
            jnp.swapaxes(gm(b_re), 2, 3), jnp.swapaxes(gm(b_im), 2, 3),
            gm(c_re), gm(c_im), jnp.swapaxes(gm(c_re), 2, 3), jnp.swapaxes(gm(c_im), 2, 3), dpq)

    def spec(v):
        blk = (1,) + v.shape[1:]
        nd = v.ndim
        return pl.BlockSpec(blk, lambda g: (g,) + (0,) * (nd - 1))

    out_shapes = [jax.ShapeDtypeStruct((g_dim, p_dim * t, p_dim * t), BF16),
                  jax.ShapeDtypeStruct((g_dim, p_dim * t, 4 * LANES), BF16),
                  jax.ShapeDtypeStruct((g_dim, 4 * LANES, p_dim * t), BF16),
                  jax.ShapeDtypeStruct((g_dim, SUBLANES, LANES), F32)]
    return pl.pallas_call(
        _s5_prep_body,
        grid=(g_dim,),
        in_specs=[spec(v) for v in args],
        out_specs=[spec(s) for s in out_shapes],
        out_shape=out_shapes,
        scratch_shapes=[pltpu.VMEM((p_dim * p_dim, 2 * t), F32)],
        compiler_params=_cparams(1),
        name="s5_prep",
    )(*args)


def _s5_body(*refs, nb, nc, with_y):
    if with_y:
        u_ref, bs_ref, coef_ref, h0_ref, toep_ref, wst_ref, y_ref, fin_ref, s_scr, h_scr = refs
    else:
        u_ref, bs_ref, coef_ref, h0_ref, fin_ref, s_scr = refs
    p_dim = u_ref.shape[1]
    t = T_CHUNK
    u = jnp.concatenate([u_ref[0, q] for q in range(p_dim)], axis=1).astype(BF16)
    s_scr[...] = jnp.dot(u, bs_ref[0], preferred_element_type=F32)
    if with_y:
        h_scr[...] = jnp.zeros(h_scr.shape, h_scr.dtype)

    af_re, af_im = coef_ref[0, 0:1, :], coef_ref[0, 1:2, :]
    ab_re, ab_im = coef_ref[0, 2:3, :], coef_ref[0, 3:4, :]
    init = tuple(h0_ref[0, r:r + 1, :] for r in range(4 * nb))

    blk = SUBLANES if nc % SUBLANES == 0 else nc

    def block(k, carry):
        new = list(carry)
        for b in range(nb):
            f0 = b * nc + blk * k
            b0 = b * nc + nc - blk - blk * k
            if blk == SUBLANES:
                f0 = pl.multiple_of(f0, SUBLANES)
                b0 = pl.multiple_of(b0, SUBLANES)
            sf_re = s_scr[pl.ds(f0, blk), 0:LANES]
            sf_im = s_scr[pl.ds(f0, blk), LANES:2 * LANES]
            sb_re = s_scr[pl.ds(b0, blk), 2 * LANES:3 * LANES]
            sb_im = s_scr[pl.ds(b0, blk), 3 * LANES:4 * LANES]
            fr, fi, br, bi = new[4 * b:4 * b + 4]
            hf_re, hf_im, hb_re, hb_im = [], [], [], []
            for j in range(blk):
                jb = blk - 1 - j
                hf_re.append(fr)
                hf_im.append(fi)
                hb_re.insert(0, br)
                hb_im.insert(0, bi)
                fr, fi = (af_re * fr - af_im * fi + sf_re[j:j + 1],
                          af_re * fi + af_im * fr + sf_im[j:j + 1])
                br, bi = (ab_re * br - ab_im * bi + sb_re[jb:jb + 1],
                          ab_re * bi + ab_im * br + sb_im[jb:jb + 1])
            if with_y:
                h_scr[pl.ds(f0, blk), 0:LANES] = jnp.concatenate(hf_re, axis=0)
                h_scr[pl.ds(f0, blk), LANES:2 * LANES] = jnp.concatenate(hf_im, axis=0)
                h_scr[pl.ds(b0, blk), 2 * LANES:3 * LANES] = jnp.concatenate(hb_re, axis=0)
                h_scr[pl.ds(b0, blk), 3 * LANES:4 * LANES] = jnp.concatenate(hb_im, axis=0)
            new[4 * b:4 * b + 4] = [fr, fi, br, bi]
        return tuple(new)

    if nc == blk:
        fin = block(0, init)
    else:
        fin = lax.fori_loop(0, nc // blk, block, init)
    fin_ref[0] = jnp.concatenate(list(fin) + [jnp.zeros((1, LANES), F32)] * (SUBLANES - 4 * nb), axis=0) \
        if 4 * nb < SUBLANES else jnp.concatenate(list(fin), axis=0)

    if with_y:
        y = jnp.dot(u, toep_ref[0], preferred_element_type=F32)
        y = y + jnp.dot(h_scr[...].astype(BF16), wst_ref[0], preferred_element_type=F32)
        y = _gelu(y)
        for p in range(p_dim):
            y_ref[0, p] = y[:, p * t:(p + 1) * t]


def _s5_call(u4, bs, coef, h0, toep=None, wst=None, *, nb, nc):
    g_dim, p_dim, rows, t = u4.shape
    with_y = toep is not None
    body = functools.partial(_s5_body, nb=nb, nc=nc, with_y=with_y)

    def spec(v):
        blk = (1,) + v.shape[1:]
        nd = v.ndim
        return pl.BlockSpec(blk, lambda g: (g,) + (0,) * (nd - 1))

    fin_shape = jax.ShapeDtypeStruct((g_dim, SUBLANES, LANES), F32)
    args = [u4, bs, coef, h0]
    scratch = [pltpu.VMEM((rows, 4 * LANES), F32)]
    if with_y:
        args += [toep, wst]
        out_shape = [jax.ShapeDtypeStruct(u4.shape, F32), fin_shape]
        scratch.append(pltpu.VMEM((rows, 4 * LANES), F32))
    else:
        out_shape = [fin_shape]
    outs = pl.pallas_call(
        body,
        grid=(g_dim,),
        in_specs=[spec(v) for v in args],
        out_specs=[spec(s) for s in out_shape],
        out_shape=out_shape,
        scratch_shapes=scratch,
        compiler_params=_cparams(1),
        name="s5_chunks" if with_y else "s5_ctx_states",
    )(*args)
    return outs


def _even_out_body(yc_ref, yt_ref, gwt_ref, gb_ref, woa_ref, wob_ref, x_ref, g1_ref, o_ref):
    yt = yt_ref[...]
    gl = jnp.dot(gwt_ref[...], yt.astype(BF16), preferred_element_type=F32) + gb_ref[...]
    ys = (yt * _sigmoid(gl)).astype(BF16)
    mix = jnp.dot(yc_ref[...], woa_ref[...], preferred_element_type=F32)
    mix = mix + jnp.dot(ys.T, wob_ref[...], preferred_element_type=F32)
    o_ref[...] = x_ref[...] + g1_ref[0] * mix


def _even_out_call(yc, yt, gwt, gb, woa, wob, x2d, g1, seq_len, tm):
    tokens, d = x2d.shape
    cw = yc.shape[1]
    sw = yt.shape[0]
    tiles_per_seq = seq_len // tm
    return pl.pallas_call(
        _even_out_body,
        grid=(tokens // tm,),
        in_specs=[pl.BlockSpec((tm, cw), lambda i: (i, 0)),
                  pl.BlockSpec((sw, tm), lambda i: (0, i)),
                  _resident((sw, sw)), _resident((sw, 1)),
                  _resident((cw, d)), _resident((sw, d)),
                  pl.BlockSpec((tm, d), lambda i: (i, 0)),
                  pl.BlockSpec((1, 1, d), lambda i: (i // tiles_per_seq, 0, 0))],
        out_specs=pl.BlockSpec((tm, d), lambda i: (i, 0)),
        out_shape=jax.ShapeDtypeStruct((tokens, d), F32),
        compiler_params=_cparams(1),
        name="even_out_proj",
    )(yc, yt, gwt, gb, woa, wob, x2d, g1)


def _odd_body(x_ref, g_ref, sh_ref, sc_ref, wu_ref, wv_ref, lg_ref, lb_ref, sw_ref, sb_ref,
              wo_ref, g1_ref, o_ref, gated_ref):
    x = x_ref[...]
    tm = x.shape[0]
    hn = _norm_mod(x, g_ref[...], sh_ref[0], sc_ref[0]).astype(BF16)
    v = _gelu(jnp.dot(hn, wv_ref[...], preferred_element_type=F32))
    v = _layer_norm(v, lg_ref[...], lb_ref[...]).astype(BF16)
    u = _gelu(jnp.dot(hn, wu_ref[...], preferred_element_type=F32))
    hd = v.shape[1] // SGU_HEADS
    for ch in range(tm // SGU_CHUNK):
        r0 = ch * SGU_CHUNK
        for h in range(SGU_HEADS):
            s = jnp.dot(sw_ref[h], v[r0:r0 + SGU_CHUNK, h * hd:(h + 1) * hd],
                        preferred_element_type=F32) + sb_ref[h]
            gated_ref[r0:r0 + SGU_CHUNK, h * hd:(h + 1) * hd] = \
                (u[r0:r0 + SGU_CHUNK, h * hd:(h + 1) * hd] * s).astype(BF16)
    mix = jnp.dot(gated_ref[...], wo_ref[...], preferred_element_type=F32)
    o_ref[...] = x + g1_ref[0] * mix


def _odd_call(x2d, g, shift, scale, wu, wv, ln_g, ln_b, sgu_w, sgu_b, wo, g1, seq_len, tm):
    tokens, d = x2d.shape
    w = wu.shape[1]
    tiles_per_seq = seq_len // tm
    mod_spec = pl.BlockSpec((1, 1, d), lambda i: (i // tiles_per_seq, 0, 0))
    return pl.pallas_call(
        _odd_body,
        grid=(tokens // tm,),
        in_specs=[pl.BlockSpec((tm, d), lambda i: (i, 0)),
                  _resident((1, d)), mod_spec, mod_spec,
                  _resident((d, w)), _resident((d, w)), _resident((1, w)), _resident((1, w)),
                  _resident(sgu_w.shape), _resident(sgu_b.shape),
                  _resident((w, d)), mod_spec],
        out_specs=pl.BlockSpec((tm, d), lambda i: (i, 0)),
        out_shape=jax.ShapeDtypeStruct((tokens, d), F32),
        scratch_shapes=[pltpu.VMEM((tm, w), BF16)],
        compiler_params=_cparams(1),
        name="odd_gmlp",
    )(x2d, g, shift, scale, wu, wv, ln_g, ln_b, sgu_w, sgu_b, wo, g1)


def _row_from_left(v):
    n_vr = GRID_W // SUBLANES
    first = lax.broadcasted_iota(jnp.int32, (SUBLANES, LANES), 0) == 0
    r = [pltpu.roll(v[SUBLANES * k:SUBLANES * (k + 1)], 1, axis=0) for k in range(n_vr)]
    return jnp.concatenate([jnp.where(first, r[k - 1] if k else 0.0, r[k]) for k in range(n_vr)], axis=0)


def _row_from_right(v):
    n_vr = GRID_W // SUBLANES
    last = lax.broadcasted_iota(jnp.int32, (SUBLANES, LANES), 0) == SUBLANES - 1
    r = [pltpu.roll(v[SUBLANES * k:SUBLANES * (k + 1)], SUBLANES - 1, axis=0) for k in range(n_vr)]
    return jnp.concatenate([jnp.where(last, r[k + 1] if k < n_vr - 1 else 0.0, r[k])
                            for k in range(n_vr)], axis=0)


def _conv3x3_row(up, mid, dn, cw_ref, cb_ref, c0):
    w = lambda k: cw_ref[pl.ds(k, 1), pl.ds(c0, LANES)]
    left = up * w(0) + mid * w(3) + dn * w(6)
    cent = up * w(1) + mid * w(4) + dn * w(7)
    right = up * w(2) + mid * w(5) + dn * w(8)
    return cent + _row_from_left(left) + _row_from_right(right) + cb_ref[:, pl.ds(c0, LANES)]


def _ffn_up_body(x_ref, g_ref, sh_ref, sc_ref, w_ref, cw_ref, cb_ref, act_ref, zb_ref, hn_ref, z_ref):
    @pl.when(pl.program_id(1) == 0)
    def _():
        hn_ref[...] = _norm_mod(x_ref[...], g_ref[...], sh_ref[0], sc_ref[0]).astype(BF16)

    z_ref[...] = jnp.dot(hn_ref[...], w_ref[...], preferred_element_type=F32)
    tm, tn = z_ref.shape
    half = tn // 2
    n_rows = tm // GRID_W
    n_cb = half // LANES

    for e, r in enumerate((0, 1, n_rows - 2, n_rows - 1)):
        zb_ref[0, e] = z_ref[r * GRID_W:(r + 1) * GRID_W, :].astype(zb_ref.dtype)
    act_ref[0:GRID_W, :] = jnp.zeros((GRID_W, half), act_ref.dtype)
    act_ref[tm - GRID_W:tm, :] = jnp.zeros((GRID_W, half), act_ref.dtype)

    def item(r, carry):
        r0 = pl.multiple_of((1 + r) * GRID_W, GRID_W)

        def conv(c0):
            rows = [z_ref[pl.ds(r0 + dy * GRID_W, GRID_W), c0:c0 + LANES] for dy in (-1, 0, 1)]
            return _conv3x3_row(*rows, cw_ref, cb_ref, c0)

        for cb in range(n_cb):
            a = conv(cb * LANES)
            g = conv(half + cb * LANES)
            act_ref[pl.ds(r0, GRID_W), cb * LANES:(cb + 1) * LANES] = (a * _silu(g)).astype(act_ref.dtype)
        return carry

    lax.fori_loop(0, n_rows - 2, item, 0)


def _ffn_up_call(x2d, g, shift, scale, w, cw, cb, seq_len, tm, tn):
    tokens, d = x2d.shape
    n2 = w.shape[1]
    half = tn // 2
    tiles_per_seq = seq_len // tm
    mod_spec = pl.BlockSpec((1, 1, d), lambda i, j: (i // tiles_per_seq, 0, 0))
    return pl.pallas_call(
        _ffn_up_body,
        grid=(tokens // tm, n2 // tn),
        in_specs=[pl.BlockSpec((tm, d), lambda i, j: (i, 0)),
                  _resident((1, d)), mod_spec, mod_spec,
                  pl.BlockSpec((d, tn), lambda i, j: (0, j)),
                  pl.BlockSpec((16, tn), lambda i, j: (0, j)),
                  pl.BlockSpec((1, tn), lambda i, j: (0, j))],
        out_specs=[pl.BlockSpec((tm, half), lambda i, j: (i, j)),
                   pl.BlockSpec((1, 4, GRID_W, tn), lambda i, j: (i, 0, 0, j))],
        out_shape=[jax.ShapeDtypeStruct((tokens, n2 // 2), BF16),
                   jax.ShapeDtypeStruct((tokens // tm, 4, GRID_W, n2), BF16)],
        scratch_shapes=[pltpu.VMEM((tm, d), BF16), pltpu.VMEM((tm, tn), F32)],
        compiler_params=_cparams(2),
        name="ffn_up_conv",
    )(x2d, g, shift, scale, w, cw, cb)


def _ffn_down_body(act_ref, zc_ref, zp_ref, zn_ref, cw_ref, cb_ref, wo_ref, x_ref, g2_ref, fg_ref, o_ref,
                   acts_ref, *, td, tn, down_per_up, up_per_seq, final_norm):
    k = pl.program_id(0)
    up_tile = k // down_per_up
    pos = k % down_per_up
    top_ok = jnp.where((up_tile % up_per_seq) == 0, 0.0, 1.0)
    bot_ok = jnp.where((up_tile % up_per_seq) == up_per_seq - 1, 0.0, 1.0)
    dff = acts_ref.shape[1]
    half = tn // 2
    acts_ref[...] = act_ref[...]

    def edge_row(rows, dst):
        def blk(j, carry):
            for s in range(half // LANES):
                cz = pl.multiple_of(j * tn + s * LANES, LANES)
                cg = pl.multiple_of(cz + half, LANES)
                a = _conv3x3_row(*rows(cz), cw_ref, cb_ref, cz)
                g = _conv3x3_row(*rows(cg), cw_ref, cb_ref, cg)
                acts_ref[dst:dst + GRID_W, pl.ds(pl.multiple_of(j * half + s * LANES, LANES), LANES)] = \
                    (a * _silu(g)).astype(acts_ref.dtype)
            return carry

        lax.fori_loop(0, 2 * dff // tn, blk, 0)

    @pl.when(pos == 0)
    def _():
        edge_row(lambda c: (zp_ref[0, 0, :, pl.ds(c, LANES)].astype(F32) * top_ok,
                            zc_ref[0, 0, :, pl.ds(c, LANES)].astype(F32),
                            zc_ref[0, 1, :, pl.ds(c, LANES)].astype(F32)), 0)

    @pl.when(pos == down_per_up - 1)
    def _():
        edge_row(lambda c: (zc_ref[0, 2, :, pl.ds(c, LANES)].astype(F32),
                            zc_ref[0, 3, :, pl.ds(c, LANES)].astype(F32),
                            zn_ref[0, 0, :, pl.ds(c, LANES)].astype(F32) * bot_ok), td - GRID_W)

    y = x_ref[...] + g2_ref[0] * jnp.dot(acts_ref[...], wo_ref[...], preferred_element_type=F32)
    if final_norm:
        ms = jnp.mean(y * y, axis=-1, keepdims=True)
        y = y * lax.rsqrt(ms + RMS_EPS) * fg_ref[...]
    o_ref[...] = y


def _ffn_down_call(act, zb, cw, cb, wo, x2d, g2, final_g, seq_len, tm, tn, td, final_norm):
    tokens, d = x2d.shape
    dff = act.shape[1]
    n2 = zb.shape[3]
    n_up = zb.shape[0]
    down_per_up = tm // td
    tiles_per_seq = seq_len // td
    body = functools.partial(_ffn_down_body, td=td, tn=tn, down_per_up=down_per_up,
                             up_per_seq=seq_len // tm, final_norm=final_norm)
    return pl.pallas_call(
        body,
        grid=(tokens // td,),
        in_specs=[pl.BlockSpec((td, dff), lambda k: (k, 0)),
                  pl.BlockSpec((1, 4, GRID_W, n2), lambda k: (k // down_per_up, 0, 0, 0)),
                  pl.BlockSpec((1, 1, GRID_W, n2),
                               lambda k: (jnp.maximum(k // down_per_up - 1, 0), 3, 0, 0)),
                  pl.BlockSpec((1, 1, GRID_W, n2),
                               lambda k: (jnp.minimum(k // down_per_up + 1, n_up - 1), 0, 0, 0)),
                  _resident((16, n2)), _resident((1, n2)), _resident((dff, d)),
                  pl.BlockSpec((td, d), lambda k: (k, 0)),
                  pl.BlockSpec((1, 1, d), lambda k: (k // tiles_per_seq, 0, 0)),
                  _resident((1, d))],
        out_specs=pl.BlockSpec((td, d), lambda k: (k, 0)),
        out_shape=jax.ShapeDtypeStruct((tokens, d), F32),
        scratch_shapes=[pltpu.VMEM((td, dff), BF16)],
        compiler_params=_cparams(1),
        name="ffn_edge_down",
    )(act, zb, zb, zb, cw, cb, wo, x2d, g2, final_g.reshape(1, d))


def _pick_tile(seq_len, want):
    tm = min(want, seq_len)
    assert seq_len % tm == 0
    return tm


def kernel(x, c, ctx, c_ctx, ada_w, ada_b, norm_g, final_g, e_w_in, e_conv_w, e_conv_b, e_ln_g, e_ln_b, s5_a_re, s5_a_im, s5_log_step, s5_b_re, s5_b_im, s5_c_re, s5_c_im, s5_d, s5_glu_w, s5_glu_b, e_w_out, o_w_in, o_ln_g, o_ln_b, o_sgu_w, o_sgu_b, o_w_out, f_w_in, f_conv_w, f_conv_b, f_w_out):
    bn, seq_len, d = x.shape
    ctx_len = ctx.shape[1]
    depth = ada_w.shape[0]
    assert depth == 2 and bn == 2, "block structure is specialised to depth 2, batch 2"
    tokens = bn * seq_len
    cw = e_conv_w.shape[2]
    sw = s5_d.shape[1]
    g_dim = sw // SSM_P
    t = T_CHUNK
    nc = seq_len // t
    nc_ctx = ctx_len // t
    assert seq_len % t == 0 and ctx_len % t == 0 and seq_len % GRID_W == 0

    x2d = x.reshape(tokens, d)

    s_rows = jnp.zeros((SUBLANES, d), F32).at[:bn].set(c).at[bn].set(c_ctx)
    mods = _ada_call(s_rows, ada_w, ada_b)

    def lat_mod(layer, k):
        return mods[layer, :bn, k * d:(k + 1) * d].reshape(bn, 1, d)

    li = 0
    w_in = e_w_in[li]
    wv = w_in[:, :cw].astype(BF16)
    wg = w_in[:, cw:2 * cw].astype(BF16)
    wut = w_in[:, 2 * cw:].T.astype(BF16)
    toep, bs, wst, coef = _s5_prep_call(s5_a_re[li], s5_a_im[li], s5_log_step[li], s5_b_re[li],
                                        s5_b_im[li], s5_c_re[li], s5_c_im[li], s5_d[li])

    g_row = norm_g[0, 0].reshape(1, d)
    ctx_shift = mods[0, bn:bn + 1, 0:d]
    ctx_scale = mods[0, bn:bn + 1, d:2 * d]
    uct = _ctx_in_call(ctx.reshape(bn * ctx_len, d), g_row, ctx_shift, ctx_scale, wut)
    uc4 = uct.reshape(g_dim, SSM_P, bn * nc_ctx, t)
    rows_ctx = -(-(bn * nc_ctx) // SUBLANES) * SUBLANES
    uc4 = jnp.pad(uc4, ((0, 0), (0, 0), (0, rows_ctx - bn * nc_ctx), (0, 0)))
    zero_h0 = jnp.zeros((g_dim, SUBLANES, LANES), F32)
    (h0s,) = _s5_call(uc4, bs, coef, zero_h0, nb=bn, nc=nc_ctx)

    tm_in = _pick_tile(seq_len, 512)
    z, ut = _even_in_call(x2d, g_row, lat_mod(0, 0), lat_mod(0, 1), wv, wg, wut, seq_len, tm_in)
    yc = _conv_call(z, e_conv_w[li], e_conv_b[li], e_ln_g[li], e_ln_b[li], seq_len,
                    _pick_tile(seq_len, 256))
    u4 = ut.reshape(g_dim, SSM_P, bn * nc, t)
    y4, _ = _s5_call(u4, bs, coef, h0s, toep, wst, nb=bn, nc=nc)
    yt = y4.reshape(sw, tokens)
    w_out = e_w_out[li]
    h = _even_out_call(yc, yt, s5_glu_w[li].T.astype(BF16), s5_glu_b[li].reshape(sw, 1),
                       w_out[:cw].astype(BF16), w_out[cw:].astype(BF16), x2d, lat_mod(0, 2),
                       seq_len, _pick_tile(seq_len, 512))

    def conv_ffn(h, layer, final_norm):
        tm = _pick_tile(seq_len, 1024)
        tn = 1024
        td = _pick_tile(tm, 256)
        n2 = f_w_in.shape[2]

        def regroup(v):
            lead = v.shape[:-1]
            v = v.reshape(lead + (2, n2 // tn, tn // 2))
            return jnp.swapaxes(v, -3, -2).reshape(lead + (n2,))

        cw = jnp.pad(regroup(f_conv_w[layer].reshape(9, n2)), ((0, 16 - 9), (0, 0)))
        cb = regroup(f_conv_b[layer]).reshape(1, n2)
        act, zb = _ffn_up_call(h, norm_g[layer, 1].reshape(1, d), lat_mod(layer, 3), lat_mod(layer, 4),
                               regroup(f_w_in[layer]).astype(BF16), cw, cb, seq_len, tm, tn)
        return _ffn_down_call(act, zb, cw, cb, f_w_out[layer].astype(BF16), h, lat_mod(layer, 5),
                              final_g, seq_len, tm, tn, td, final_norm)

    h = conv_ffn(h, 0, False)

    li = 0
    w = o_w_in.shape[2] // 2
    h = _odd_call(h, norm_g[1, 0].reshape(1, d), lat_mod(1, 0), lat_mod(1, 1),
                  o_w_in[li][:, :w].astype(BF16), o_w_in[li][:, w:].astype(BF16),
                  o_ln_g[li].reshape(1, w), o_ln_b[li].reshape(1, w),
                  o_sgu_w[li].astype(BF16), o_sgu_b[li].reshape(SGU_HEADS, SGU_CHUNK, 1),
                  o_w_out[li].astype(BF16), lat_mod(1, 2), seq_len, _pick_tile(seq_len, 256))
    h = conv_ffn(h, 1, True)
    return h.reshape(bn, seq_len, d)
```

```python
import functools
import math

import jax
import jax.numpy as jnp
from jax import lax
from jax.experimental import pallas as pl
from jax.experimental.pallas import tpu as pltpu

F32 = jnp.float32
BF16 = jnp.bfloat16
HIGHEST = lax.Precision.HIGHEST

RMS_EPS = 1e-6
LN_EPS = 1e-5
GRID_W = 64
CONV_K = 31
CONV_HALO = 16
SSM_P = 16
SSM_N = 64
SGU_CHUNK = 128
SGU_HEADS = 8
T_CHUNK = 128
LANES = 128
SUBLANES = 8
VMEM_LIMIT_BYTES = 56 * 1024 * 1024


def _cparams(ngrid):
    return pltpu.CompilerParams(dimension_semantics=("arbitrary",) * ngrid,
                                vmem_limit_bytes=VMEM_LIMIT_BYTES)


def _resident(shape):
    nd = len(shape)
    return pl.BlockSpec(shape, lambda *_: (0,) * nd, pipeline_mode=pl.Buffered(1))


def _sigmoid(x):
    return 1.0 / (1.0 + jnp.exp(-x))


def _silu(x):
    return x * _sigmoid(x)


def _gelu(x):
    return 0.5 * x * (1.0 + jnp.tanh(math.sqrt(2.0 / math.pi) * (x + 0.044715 * (x * x * x))))


def _norm_mod(x, g, shift, scale):
    ms = jnp.mean(x * x, axis=-1, keepdims=True)
    y = x * lax.rsqrt(ms + RMS_EPS) * g
    return y * (1.0 + scale) + shift


def _layer_norm(x, g, b):
    mu = jnp.mean(x, axis=-1, keepdims=True)
    xc = x - mu
    var = jnp.mean(xc * xc, axis=-1, keepdims=True)
    return xc * lax.rsqrt(var + LN_EPS) * g + b


def _ada_body(s_ref, w_ref, b_ref, o_ref):
    s = _silu(s_ref[...])
    o_ref[0] = jnp.dot(s, w_ref[0], preferred_element_type=F32, precision=HIGHEST) + b_ref[0]


def _ada_call(s_rows, ada_w, ada_b):
    depth, d, n = ada_w.shape
    tn = 1024
    return pl.pallas_call(
        _ada_body,
        grid=(depth, n // tn),
        in_specs=[pl.BlockSpec((SUBLANES, d), lambda l, j: (0, 0)),
                  pl.BlockSpec((1, d, tn), lambda l, j: (l, 0, j)),
                  pl.BlockSpec((1, 1, tn), lambda l, j: (l, 0, j))],
        out_specs=pl.BlockSpec((1, SUBLANES, tn), lambda l, j: (l, 0, j)),
        out_shape=jax.ShapeDtypeStruct((depth, SUBLANES, n), F32),
        compiler_params=_cparams(2),
        name="ada_mod",
    )(s_rows, ada_w, ada_b.reshape(depth, 1, n))


def _even_in_body(x_ref, g_ref, sh_ref, sc_ref, wv_ref, wg_ref, wut_ref, z_ref, ut_ref):
    hn = _norm_mod(x_ref[...], g_ref[...], sh_ref[0], sc_ref[0]).astype(BF16)
    av = jnp.dot(hn, wv_ref[...], preferred_element_type=F32)
    ag = jnp.dot(hn, wg_ref[...], preferred_element_type=F32)
    z_ref[...] = av * _sigmoid(ag)
    ut_ref[...] = lax.dot_general(wut_ref[...], hn, (((1,), (1,)), ((), ())),
                                  preferred_element_type=F32)


def _even_in_call(x2d, g, shift, scale, wv, wg, wut, seq_len, tm):
    tokens, d = x2d.shape
    cw = wv.shape[1]
    sw = wut.shape[0]
    tiles_per_seq = seq_len // tm
    mod_spec = pl.BlockSpec((1, 1, d), lambda i: (i // tiles_per_seq, 0, 0))
    return pl.pallas_call(
        _even_in_body,
        grid=(tokens // tm,),
        in_specs=[pl.BlockSpec((tm, d), lambda i: (i, 0)),
                  _resident((1, d)), mod_spec, mod_spec,
                  _resident((d, cw)), _resident((d, cw)), _resident((sw, d))],
        out_specs=[pl.BlockSpec((tm, cw), lambda i: (i, 0)),
                   pl.BlockSpec((sw, tm), lambda i: (0, i))],
        out_shape=[jax.ShapeDtypeStruct((tokens, cw), F32),
                   jax.ShapeDtypeStruct((sw, tokens), F32)],
        compiler_params=_cparams(1),
        name="even_in_proj",
    )(x2d, g, shift, scale, wv, wg, wut)


def _ctx_in_body(x_ref, g_ref, sh_ref, sc_ref, wut_ref, ut_ref):
    hn = _norm_mod(x_ref[...], g_ref[...], sh_ref[...], sc_ref[...]).astype(BF16)
    ut_ref[...] = lax.dot_general(wut_ref[...], hn, (((1,), (1,)), ((), ())),
                                  preferred_element_type=F32)


def _ctx_in_call(c2d, g, shift, scale, wut):
    tokens, d = c2d.shape
    sw = wut.shape[0]
    return pl.pallas_call(
        _ctx_in_body,
        out_shape=jax.ShapeDtypeStruct((sw, tokens), F32),
        compiler_params=pltpu.CompilerParams(vmem_limit_bytes=VMEM_LIMIT_BYTES),
        name="ctx_in_proj",
    )(c2d, g, shift, scale, wut)


CONV_ROWS = 128


def _conv_body(zp_ref, zm_ref, zn_ref, w_ref, b_ref, g_ref, beta_ref, o_ref, buf_ref, acc_ref,
               *, tl, tiles_per_seq):
    i = pl.program_id(0)
    first = (i % tiles_per_seq) == 0
    last = (i % tiles_per_seq) == tiles_per_seq - 1
    buf_ref[0:CONV_HALO, :] = jnp.where(first, 0.0, zp_ref[...])
    buf_ref[CONV_HALO:CONV_HALO + tl, :] = zm_ref[...]
    buf_ref[CONV_HALO + tl:2 * CONV_HALO + tl, :] = jnp.where(last, 0.0, zn_ref[...])

    cw = o_ref.shape[1]
    n_rb = tl // CONV_ROWS
    n_cb = cw // LANES
    rows = CONV_ROWS + SUBLANES

    def lane_block(r0, c0):
        acc = jnp.zeros((CONV_ROWS, LANES), F32)
        for r in range(SUBLANES):
            part = None
            for a in range(4):
                k = SUBLANES * a + r - 1
                if k < 0 or k >= CONV_K:
                    continue
                term = buf_ref[pl.ds(r0 + SUBLANES * a, rows), c0:c0 + LANES] * \
                    w_ref[k:k + 1, c0:c0 + LANES]
                part = term if part is None else part + term
            if r:
                part = pltpu.roll(part, rows - r, axis=0)
            acc = acc + part[:CONV_ROWS]
        acc_ref[pl.ds(r0, CONV_ROWS), c0:c0 + LANES] = acc + b_ref[:, c0:c0 + LANES]

    def block(rb, carry):
        r0 = pl.multiple_of(rb * CONV_ROWS, CONV_ROWS)
        for cb in range(n_cb):
            lane_block(r0, cb * LANES)
        return carry

    lax.fori_loop(0, n_rb, block, 0)

    ln_rows = 128

    def ln_block(rb, carry):
        r0 = pl.multiple_of(rb * ln_rows, ln_rows)
        y = _layer_norm(acc_ref[pl.ds(r0, ln_rows), :], g_ref[...], beta_ref[...])
        o_ref[pl.ds(r0, ln_rows), :] = _silu(y).astype(o_ref.dtype)
        return carry

    lax.fori_loop(0, tl // ln_rows, ln_block, 0)


def _conv_call(z2d, conv_w, conv_b, ln_g, ln_b, seq_len, tl):
    tokens, cw = z2d.shape
    tiles_per_seq = seq_len // tl
    hb = tl // CONV_HALO
    n_hblocks = tokens // CONV_HALO
    w_pad = jnp.pad(conv_w, ((0, 32 - CONV_K), (0, 0)))
    body = functools.partial(_conv_body, tl=tl, tiles_per_seq=tiles_per_seq)
    return pl.pallas_call(
        body,
        grid=(tokens // tl,),
        in_specs=[pl.BlockSpec((CONV_HALO, cw), lambda i: (jnp.maximum(i * hb - 1, 0), 0)),
                  pl.BlockSpec((tl, cw), lambda i: (i, 0)),
                  pl.BlockSpec((CONV_HALO, cw), lambda i: (jnp.minimum((i + 1) * hb, n_hblocks - 1), 0)),
                  _resident((32, cw)), _resident((1, cw)), _resident((1, cw)), _resident((1, cw))],
        out_specs=pl.BlockSpec((tl, cw), lambda i: (i, 0)),
        out_shape=jax.ShapeDtypeStruct((tokens, cw), BF16),
        scratch_shapes=[pltpu.VMEM((tl + 2 * CONV_HALO, cw), F32), pltpu.VMEM((tl, cw), F32)],
        compiler_params=_cparams(1),
        name="conformer_conv",
    )(z2d, z2d, z2d, w_pad, conv_b.reshape(1, cw), ln_g.reshape(1, cw), ln_b.reshape(1, cw))


def _s5_prep_body(ar_row_ref, ai_row_ref, ar_col_ref, ai_col_ref, ls_ref,
                  bt_re_ref, bt_im_ref, c_re_ref, c_im_ref, ct_re_ref, ct_im_ref, dpq_ref,
                  toep_ref, bs_ref, wst_ref, coef_ref, kv_ref):
    t = T_CHUNK
    n = SSM_N
    p_dim = SSM_P
    lane2 = lax.broadcasted_iota(jnp.int32, (1, 2 * t), 1)
    lane1 = lax.broadcasted_iota(jnp.int32, (1, t), 1).astype(F32)
    sub1 = lax.broadcasted_iota(jnp.int32, (t, 1), 0).astype(F32)

    bs_ref[...] = jnp.zeros(bs_ref.shape, bs_ref.dtype)
    wst_ref[...] = jnp.zeros(wst_ref.shape, wst_ref.dtype)
    coef_ref[...] = jnp.zeros(coef_ref.shape, coef_ref.dtype)

    kv = dpq_ref[0] * (lane2 == t).astype(F32)
    for d in range(2):
        dt = jnp.exp(ls_ref[0, d])
        ar = ar_row_ref[0, d]
        ai = ai_row_ref[0, d]
        lam_row = ar * dt
        ang_row = ai * dt
        lam_col = ar_col_ref[0, d] * dt
        ang_col = ai_col_ref[0, d] * dt
        mag = jnp.exp(lam_row)
        ab_re = mag * jnp.cos(ang_row)
        ab_im = mag * jnp.sin(ang_row)
        den = ar * ar + ai * ai
        num_re = ab_re - 1.0
        f_re = (num_re * ar + ab_im * ai) / den
        f_im = (ab_im * ar - num_re * ai) / den
        bbt_re = f_re * bt_re_ref[0, d] - f_im * bt_im_ref[0, d]
        bbt_im = f_re * bt_im_ref[0, d] + f_im * bt_re_ref[0, d]

        if d == 0:
            lag2 = jnp.maximum(lane2 - t, 0).astype(F32)
            msk2 = (lane2 >= t).astype(F32)
        else:
            lag2 = jnp.maximum(t - lane2, 0).astype(F32)
            msk2 = (lane2 <= t).astype(F32)
        pm = jnp.exp(lam_col * lag2) * msk2
        pt_re = pm * jnp.cos(ang_col * lag2)
        pt_im = pm * jnp.sin(ang_col * lag2)
        cb_re = jnp.concatenate(
            [c_re_ref[0, d, pl.ds(p, 1), :] * bbt_re - c_im_ref[0, d, pl.ds(p, 1), :] * bbt_im
             for p in range(p_dim)], axis=0)
        cb_im = jnp.concatenate(
            [c_re_ref[0, d, pl.ds(p, 1), :] * bbt_im + c_im_ref[0, d, pl.ds(p, 1), :] * bbt_re
             for p in range(p_dim)], axis=0)
        kv = kv + (jnp.dot(cb_re, pt_re, preferred_element_type=F32, precision=HIGHEST)
                   - jnp.dot(cb_im, pt_im, preferred_element_type=F32, precision=HIGHEST))

        lag_col = (t - 1.0 - sub1) if d == 0 else sub1
        pw_m = jnp.exp(lag_col * lam_row)
        pw_re = pw_m * jnp.cos(lag_col * ang_row)
        pw_im = pw_m * jnp.sin(lag_col * ang_row)
        for q in range(p_dim):
            br = bbt_re[q:q + 1, :]
            bi = bbt_im[q:q + 1, :]
            bs_ref[0, q * t:(q + 1) * t, (2 * d) * LANES:(2 * d) * LANES + n] = \
                (pw_re * br - pw_im * bi).astype(bs_ref.dtype)
            bs_ref[0, q * t:(q + 1) * t, (2 * d + 1) * LANES:(2 * d + 1) * LANES + n] = \
                (pw_re * bi + pw_im * br).astype(bs_ref.dtype)

        lag_row = (lane1 + 1.0) if d == 0 else (t - lane1)
        qm = jnp.exp(lam_col * lag_row)
        q_re = qm * jnp.cos(ang_col * lag_row)
        q_im = qm * jnp.sin(ang_col * lag_row)
        for p in range(p_dim):
            cr = ct_re_ref[0, d, :, p:p + 1]
            ci = ct_im_ref[0, d, :, p:p + 1]
            wst_ref[0, (2 * d) * LANES:(2 * d) * LANES + n, p * t:(p + 1) * t] = \
                (cr * q_re - ci * q_im).astype(wst_ref.dtype)
            wst_ref[0, (2 * d + 1) * LANES:(2 * d + 1) * LANES + n, p * t:(p + 1) * t] = \
                (-(cr * q_im + ci * q_re)).astype(wst_ref.dtype)

        tt = float(t)
        at_m = jnp.exp(lam_row * tt)
        coef_ref[0, 2 * d:2 * d + 1, 0:n] = at_m * jnp.cos(ang_row * tt)
        coef_ref[0, 2 * d + 1:2 * d + 2, 0:n] = at_m * jnp.sin(ang_row * tt)

    kv_ref[...] = kv

    def toep_block(p, carry):
        kv_p = kv_ref[pl.ds(pl.multiple_of(p * p_dim, p_dim), p_dim), :]
        for q in range(p_dim):
            row = jnp.broadcast_to(kv_p[q:q + 1, :], (t, 2 * t))
            rolled = pltpu.roll(row, 0, axis=1, stride=1, stride_axis=0)
            toep_ref[0, q * t:(q + 1) * t, pl.ds(pl.multiple_of(p * t, t), t)] = \
                rolled[:, t:].astype(toep_ref.dtype)
        return carry

    lax.fori_loop(0, p_dim, toep_block, 0)


def _s5_prep_call(a_re, a_im, log_step, b_re, b_im, c_re, c_im, d_skip):
    g_dim = a_re.shape[1]
    n, p_dim, t = SSM_N, SSM_P, T_CHUNK
    gm = lambda v: jnp.swapaxes(v, 0, 1)
    ar, ai = gm(a_re), gm(a_im)
    dpq = (jnp.eye(p_dim, dtype=F32)[None] * d_skip.reshape(g_dim, p_dim, 1)).reshape(g_dim, p_dim * p_dim, 1)
    args = (ar[:, :, None, :], ai[:, :, None, :], ar[:, :, :, None], ai[:, :, :, None],
            gm(log_step)[:, :, None, None],
            jnp.swapaxes(gm(b_re), 2, 3), jnp.swapaxes(gm(b_im), 2, 3),
            gm(c_re), gm(c_im), jnp.swapaxes(gm(c_re), 2, 3), jnp.swapaxes(gm(c_im), 2, 3), dpq)

    def spec(v):
        blk = (1,) + v.shape[1:]
        nd = v.ndim
        return pl.BlockSpec(blk, lambda g: (g,) + (0,) * (nd - 1))

    out_shapes = [jax.ShapeDtypeStruct((g_dim, p_dim * t, p_dim * t), BF16),
                  jax.ShapeDtypeStruct((g_dim, p_dim * t, 4 * LANES), BF16),
                  jax.ShapeDtypeStruct((g_dim, 4 * LANES, p_dim * t), BF16),
                  jax.ShapeDtypeStruct((g_dim, SUBLANES, LANES), F32)]
    return pl.pallas_call(
        _s5_prep_body,
        grid=(g_dim,),
        in_specs=[spec(v) for v in args],
        out_specs=[spec(s) for s in out_shapes],
        out_shape=out_shapes,
        scratch_shapes=[pltpu.VMEM((p_dim * p_dim, 2 * t), F32)],
        compiler_params=_cparams(1),
        name="s5_prep",
    )(*args)


def _s5_body(*refs, nb, nc, with_y):
    if with_y:
        u_ref, bs_ref, coef_ref, h0_ref, toep_ref, wst_ref, y_ref, fin_ref, s_scr, h_scr = refs
    else:
        u_ref, bs_ref, coef_ref, h0_ref, fin_ref, s_scr = refs
    p_dim = u_ref.shape[1]
    t = T_CHUNK
    u = jnp.concatenate([u_ref[0, q] for q in range(p_dim)], axis=1).astype(BF16)
    s_scr[...] = jnp.dot(u, bs_ref[0], preferred_element_type=F32)
    if with_y:
        h_scr[...] = jnp.zeros(h_scr.shape, h_scr.dtype)

    af_re, af_im = coef_ref[0, 0:1, :], coef_ref[0, 1:2, :]
    ab_re, ab_im = coef_ref[0, 2:3, :], coef_ref[0, 3:4, :]
    init = tuple(h0_ref[0, r:r + 1, :] for r in range(4 * nb))

    blk = SUBLANES if nc % SUBLANES == 0 else nc

    def block(k, carry):
        new = list(carry)
        for b in range(nb):
            f0 = b * nc + blk * k
            b0 = b * nc + nc - blk - blk * k
            if blk == SUBLANES:
                f0 = pl.multiple_of(f0, SUBLANES)
                b0 = pl.multiple_of(b0, SUBLANES)
            sf_re = s_scr[pl.ds(f0, blk), 0:LANES]
            sf_im = s_scr[pl.ds(f0, blk), LANES:2 * LANES]
            sb_re = s_scr[pl.ds(b0, blk), 2 * LANES:3 * LANES]
            sb_im = s_scr[pl.ds(b0, blk), 3 * LANES:4 * LANES]
            fr, fi, br, bi = new[4 * b:4 * b + 4]
            hf_re, hf_im, hb_re, hb_im = [], [], [], []
            for j in range(blk):
                jb = blk - 1 - j
                hf_re.append(fr)
                hf_im.append(fi)
                hb_re.insert(0, br)
                hb_im.insert(0, bi)
                fr, fi = (af_re * fr - af_im * fi + sf_re[j:j + 1],
                          af_re * fi + af_im * fr + sf_im[j:j + 1])
                br, bi = (ab_re * br - ab_im * bi + sb_re[jb:jb + 1],
                          ab_re * bi + ab_im * br + sb_im[jb:jb + 1])
            if with_y:
                h_scr[pl.ds(f0, blk), 0:LANES] = jnp.concatenate(hf_re, axis=0)
                h_scr[pl.ds(f0, blk), LANES:2 * LANES] = jnp.concatenate(hf_im, axis=0)
                h_scr[pl.ds(b0, blk), 2 * LANES:3 * LANES] = jnp.concatenate(hb_re, axis=0)
                h_scr[pl.ds(b0, blk), 3 * LANES:4 * LANES] = jnp.concatenate(hb_im, axis=0)
            new[4 * b:4 * b + 4] = [fr, fi, br, bi]
        return tuple(new)

    if nc == blk:
        fin = block(0, init)
    else:
        fin = lax.fori_loop(0, nc // blk, block, init)
    fin_ref[0] = jnp.concatenate(list(fin) + [jnp.zeros((1, LANES), F32)] * (SUBLANES - 4 * nb), axis=0) \
        if 4 * nb < SUBLANES else jnp.concatenate(list(fin), axis=0)

    if with_y:
        y = jnp.dot(u, toep_ref[0], preferred_element_type=F32)
        y = y + jnp.dot(h_scr[...].astype(BF16), wst_ref[0], preferred_element_type=F32)
        y = _gelu(y)
        for p in range(p_dim):
            y_ref[0, p] = y[:, p * t:(p + 1) * t]


def _s5_call(u4, bs, coef, h0, toep=None, wst=None, *, nb, nc):
    g_dim, p_dim, rows, t = u4.shape
    with_y = toep is not None
    body = functools.partial(_s5_body, nb=nb, nc=nc, with_y=with_y)

    def spec(v):
        blk = (1,) + v.shape[1:]
        nd = v.ndim
        return pl.BlockSpec(blk, lambda g: (g,) + (0,) * (nd - 1))

    fin_shape = jax.ShapeDtypeStruct((g_dim, SUBLANES, LANES), F32)
    args = [u4, bs, coef, h0]
    scratch = [pltpu.VMEM((rows, 4 * LANES), F32)]
    if with_y:
        args += [toep, wst]
        out_shape = [jax.ShapeDtypeStruct(u4.shape, F32), fin_shape]
        scratch.append(pltpu.VMEM((rows, 4 * LANES), F32))
    else:
        out_shape = [fin_shape]
    outs = pl.pallas_call(
        body,
        grid=(g_dim,),
        in_specs=[spec(v) for v in args],
        out_specs=[spec(s) for s in out_shape],
        out_shape=out_shape,
        scratch_shapes=scratch,
        compiler_params=_cparams(1),
        name="s5_chunks" if with_y else "s5_ctx_states",
    )(*args)
    return outs


def _even_out_body(yc_ref, yt_ref, gwt_ref, gb_ref, woa_ref, wob_ref, x_ref, g1_ref, o_ref):
    yt = yt_ref[...]
    gl = jnp.dot(gwt_ref[...], yt.astype(BF16), preferred_element_type=F32) + gb_ref[...]
    ys = (yt * _sigmoid(gl)).astype(BF16)
    mix = jnp.dot(yc_ref[...], woa_ref[...], preferred_element_type=F32)
    mix = mix + jnp.dot(ys.T, wob_ref[...], preferred_element_type=F32)
    o_ref[...] = x_ref[...] + g1_ref[0] * mix


def _even_out_call(yc, yt, gwt, gb, woa, wob, x2d, g1, seq_len, tm):
    tokens, d = x2d.shape
    cw = yc.shape[1]
    sw = yt.shape[0]
    tiles_per_seq = seq_len // tm
    return pl.pallas_call(
        _even_out_body,
        grid=(tokens // tm,),
        in_specs=[pl.BlockSpec((tm, cw), lambda i: (i, 0)),
                  pl.BlockSpec((sw, tm), lambda i: (0, i)),
                  _resident((sw, sw)), _resident((sw, 1)),
                  _resident((cw, d)), _resident((sw, d)),
                  pl.BlockSpec((tm, d), lambda i: (i, 0)),
                  pl.BlockSpec((1, 1, d), lambda i: (i // tiles_per_seq, 0, 0))],
        out_specs=pl.BlockSpec((tm, d), lambda i: (i, 0)),
        out_shape=jax.ShapeDtypeStruct((tokens, d), F32),
        compiler_params=_cparams(1),
        name="even_out_proj",
    )(yc, yt, gwt, gb, woa, wob, x2d, g1)


def _odd_body(x_ref, g_ref, sh_ref, sc_ref, wu_ref, wv_ref, lg_ref, lb_ref, sw_ref, sb_ref,
              wo_ref, g1_ref, o_ref, gated_ref):
    x = x_ref[...]
    tm = x.shape[0]
    hn = _norm_mod(x, g_ref[...], sh_ref[0], sc_ref[0]).astype(BF16)
    v = _gelu(jnp.dot(hn, wv_ref[...], preferred_element_type=F32))
    v = _layer_norm(v, lg_ref[...], lb_ref[...]).astype(BF16)
    u = _gelu(jnp.dot(hn, wu_ref[...], preferred_element_type=F32))
    hd = v.shape[1] // SGU_HEADS
    for ch in range(tm // SGU_CHUNK):
        r0 = ch * SGU_CHUNK
        for h in range(SGU_HEADS):
            s = jnp.dot(sw_ref[h], v[r0:r0 + SGU_CHUNK, h * hd:(h + 1) * hd],
                        preferred_element_type=F32) + sb_ref[h]
            gated_ref[r0:r0 + SGU_CHUNK, h * hd:(h + 1) * hd] = \
                (u[r0:r0 + SGU_CHUNK, h * hd:(h + 1) * hd] * s).astype(BF16)
    mix = jnp.dot(gated_ref[...], wo_ref[...], preferred_element_type=F32)
    o_ref[...] = x + g1_ref[0] * mix


def _odd_call(x2d, g, shift, scale, wu, wv, ln_g, ln_b, sgu_w, sgu_b, wo, g1, seq_len, tm):
    tokens, d = x2d.shape
    w = wu.shape[1]
    tiles_per_seq = seq_len // tm
    mod_spec = pl.BlockSpec((1, 1, d), lambda i: (i // tiles_per_seq, 0, 0))
    return pl.pallas_call(
        _odd_body,
        grid=(tokens // tm,),
        in_specs=[pl.BlockSpec((tm, d), lambda i: (i, 0)),
                  _resident((1, d)), mod_spec, mod_spec,
                  _resident((d, w)), _resident((d, w)), _resident((1, w)), _resident((1, w)),
                  _resident(sgu_w.shape), _resident(sgu_b.shape),
                  _resident((w, d)), mod_spec],
        out_specs=pl.BlockSpec((tm, d), lambda i: (i, 0)),
        out_shape=jax.ShapeDtypeStruct((tokens, d), F32),
        scratch_shapes=[pltpu.VMEM((tm, w), BF16)],
        compiler_params=_cparams(1),
        name="odd_gmlp",
    )(x2d, g, shift, scale, wu, wv, ln_g, ln_b, sgu_w, sgu_b, wo, g1)


def _row_from_left(v):
    n_vr = GRID_W // SUBLANES
    first = lax.broadcasted_iota(jnp.int32, (SUBLANES, LANES), 0) == 0
    r = [pltpu.roll(v[SUBLANES * k:SUBLANES * (k + 1)], 1, axis=0) for k in range(n_vr)]
    return jnp.concatenate([jnp.where(first, r[k - 1] if k else 0.0, r[k]) for k in range(n_vr)], axis=0)


def _row_from_right(v):
    n_vr = GRID_W // SUBLANES
    last = lax.broadcasted_iota(jnp.int32, (SUBLANES, LANES), 0) == SUBLANES - 1
    r = [pltpu.roll(v[SUBLANES * k:SUBLANES * (k + 1)], SUBLANES - 1, axis=0) for k in range(n_vr)]
    return jnp.concatenate([jnp.where(last, r[k + 1] if k < n_vr - 1 else 0.0, r[k])
                            for k in range(n_vr)], axis=0)


def _conv3x3_row(up, mid, dn, cw_ref, cb_ref, c0):
    w = lambda k: cw_ref[pl.ds(k, 1), pl.ds(c0, LANES)]
    left = up * w(0) + mid * w(3) + dn * w(6)
    cent = up * w(1) + mid * w(4) + dn * w(7)
    right = up * w(2) + mid * w(5) + dn * w(8)
    return cent + _row_from_left(left) + _row_from_right(right) + cb_ref[:, pl.ds(c0, LANES)]


def _ffn_up_body(x_ref, g_ref, sh_ref, sc_ref, wa_ref, wg_ref, cw_ref, cb_ref, act_ref, zb_ref,
                 hn_ref, z_ref):
    @pl.when(pl.program_id(1) == 0)
    def _():
        hn_ref[...] = _norm_mod(x_ref[...], g_ref[...], sh_ref[0], sc_ref[0]).astype(BF16)

    tm, tn = z_ref.shape
    half = tn // 2
    n_rows = tm // GRID_W
    n_cb = half // LANES
    z_ref[:, 0:half] = jnp.dot(hn_ref[...], wa_ref[...], preferred_element_type=F32)
    z_ref[:, half:tn] = jnp.dot(hn_ref[...], wg_ref[...], preferred_element_type=F32)

    for e, r in enumerate((0, 1, n_rows - 2, n_rows - 1)):
        zb_ref[0, e] = z_ref[r * GRID_W:(r + 1) * GRID_W, :].astype(zb_ref.dtype)
    act_ref[0:GRID_W, :] = jnp.zeros((GRID_W, half), act_ref.dtype)
    act_ref[tm - GRID_W:tm, :] = jnp.zeros((GRID_W, half), act_ref.dtype)

    def item(r, carry):
        r0 = pl.multiple_of((1 + r) * GRID_W, GRID_W)

        def conv(c0):
            rows = [z_ref[pl.ds(r0 + dy * GRID_W, GRID_W), c0:c0 + LANES] for dy in (-1, 0, 1)]
            return _conv3x3_row(*rows, cw_ref, cb_ref, c0)

        for cb in range(n_cb):
            a = conv(cb * LANES)
            g = conv(half + cb * LANES)
            act_ref[pl.ds(r0, GRID_W), cb * LANES:(cb + 1) * LANES] = (a * _silu(g)).astype(act_ref.dtype)
        return carry

    lax.fori_loop(0, n_rows - 2, item, 0)


def _ffn_up_call(x2d, g, shift, scale, w, cw, cb, seq_len, tm, tn):
    tokens, d = x2d.shape
    n2 = w.shape[1]
    half = tn // 2
    gate_blk0 = (n2 // 2) // half
    tiles_per_seq = seq_len // tm
    mod_spec = pl.BlockSpec((1, 1, d), lambda i, j: (i // tiles_per_seq, 0, 0))
    return pl.pallas_call(
        _ffn_up_body,
        grid=(tokens // tm, n2 // tn),
        in_specs=[pl.BlockSpec((tm, d), lambda i, j: (i, 0)),
                  _resident((1, d)), mod_spec, mod_spec,
                  pl.BlockSpec((d, half), lambda i, j: (0, j)),
                  pl.BlockSpec((d, half), lambda i, j: (0, gate_blk0 + j)),
                  pl.BlockSpec((16, tn), lambda i, j: (0, j)),
                  pl.BlockSpec((1, tn), lambda i, j: (0, j))],
        out_specs=[pl.BlockSpec((tm, half), lambda i, j: (i, j)),
                   pl.BlockSpec((1, 4, GRID_W, tn), lambda i, j: (i, 0, 0, j))],
        out_shape=[jax.ShapeDtypeStruct((tokens, n2 // 2), BF16),
                   jax.ShapeDtypeStruct((tokens // tm, 4, GRID_W, n2), BF16)],
        scratch_shapes=[pltpu.VMEM((tm, d), BF16), pltpu.VMEM((tm, tn), F32)],
        compiler_params=_cparams(2),
        name="ffn_up_conv",
    )(x2d, g, shift, scale, w, w, cw, cb)


def _ffn_down_body(act_ref, zc_ref, zp_ref, zn_ref, cw_ref, cb_ref, wo_ref, x_ref, g2_ref, fg_ref, o_ref,
                   acts_ref, *, td, tn, down_per_up, up_per_seq, final_norm):
    k = pl.program_id(0)
    up_tile = k // down_per_up
    pos = k % down_per_up
    top_ok = jnp.where((up_tile % up_per_seq) == 0, 0.0, 1.0)
    bot_ok = jnp.where((up_tile % up_per_seq) == up_per_seq - 1, 0.0, 1.0)
    dff = acts_ref.shape[1]
    half = tn // 2
    acts_ref[...] = act_ref[...]

    def edge_row(rows, dst):
        def blk(j, carry):
            for s in range(half // LANES):
                cz = pl.multiple_of(j * tn + s * LANES, LANES)
                cg = pl.multiple_of(cz + half, LANES)
                a = _conv3x3_row(*rows(cz), cw_ref, cb_ref, cz)
                g = _conv3x3_row(*rows(cg), cw_ref, cb_ref, cg)
                acts_ref[dst:dst + GRID_W, pl.ds(pl.multiple_of(j * half + s * LANES, LANES), LANES)] = \
                    (a * _silu(g)).astype(acts_ref.dtype)
            return carry

        lax.fori_loop(0, 2 * dff // tn, blk, 0)

    @pl.when(pos == 0)
    def _():
        edge_row(lambda c: (zp_ref[0, 0, :, pl.ds(c, LANES)].astype(F32) * top_ok,
                            zc_ref[0, 0, :, pl.ds(c, LANES)].astype(F32),
                            zc_ref[0, 1, :, pl.ds(c, LANES)].astype(F32)), 0)

    @pl.when(pos == down_per_up - 1)
    def _():
        edge_row(lambda c: (zc_ref[0, 2, :, pl.ds(c, LANES)].astype(F32),
                            zc_ref[0, 3, :, pl.ds(c, LANES)].astype(F32),
                            zn_ref[0, 0, :, pl.ds(c, LANES)].astype(F32) * bot_ok), td - GRID_W)

    y = x_ref[...] + g2_ref[0] * jnp.dot(acts_ref[...], wo_ref[...], preferred_element_type=F32)
    if final_norm:
        ms = jnp.mean(y * y, axis=-1, keepdims=True)
        y = y * lax.rsqrt(ms + RMS_EPS) * fg_ref[...]
    o_ref[...] = y


def _ffn_down_call(act, zb, cw, cb, wo, x2d, g2, final_g, seq_len, tm, tn, td, final_norm):
    tokens, d = x2d.shape
    dff = act.shape[1]
    n2 = zb.shape[3]
    n_up = zb.shape[0]
    down_per_up = tm // td
    tiles_per_seq = seq_len // td
    body = functools.partial(_ffn_down_body, td=td, tn=tn, down_per_up=down_per_up,
                             up_per_seq=seq_len // tm, final_norm=final_norm)
    return pl.pallas_call(
        body,
        grid=(tokens // td,),
        in_specs=[pl.BlockSpec((td, dff), lambda k: (k, 0)),
                  pl.BlockSpec((1, 4, GRID_W, n2), lambda k: (k // down_per_up, 0, 0, 0)),
                  pl.BlockSpec((1, 1, GRID_W, n2),
                               lambda k: (jnp.maximum(k // down_per_up - 1, 0), 3, 0, 0)),
                  pl.BlockSpec((1, 1, GRID_W, n2),
                               lambda k: (jnp.minimum(k // down_per_up + 1, n_up - 1), 0, 0, 0)),
                  _resident((16, n2)), _resident((1, n2)), _resident((dff, d)),
                  pl.BlockSpec((td, d), lambda k: (k, 0)),
                  pl.BlockSpec((1, 1, d), lambda k: (k // tiles_per_seq, 0, 0)),
                  _resident((1, d))],
        out_specs=pl.BlockSpec((td, d), lambda k: (k, 0)),
        out_shape=jax.ShapeDtypeStruct((tokens, d), F32),
        scratch_shapes=[pltpu.VMEM((td, dff), BF16)],
        compiler_params=_cparams(1),
        name="ffn_edge_down",
    )(act, zb, zb, zb, cw, cb, wo, x2d, g2, final_g.reshape(1, d))


def _pick_tile(seq_len, want):
    tm = min(want, seq_len)
    assert seq_len % tm == 0
    return tm


def kernel(x, c, ctx, c_ctx, ada_w, ada_b, norm_g, final_g, e_w_in, e_conv_w, e_conv_b, e_ln_g, e_ln_b, s5_a_re, s5_a_im, s5_log_step, s5_b_re, s5_b_im, s5_c_re, s5_c_im, s5_d, s5_glu_w, s5_glu_b, e_w_out, o_w_in, o_ln_g, o_ln_b, o_sgu_w, o_sgu_b, o_w_out, f_w_in, f_conv_w, f_conv_b, f_w_out):
    bn, seq_len, d = x.shape
    ctx_len = ctx.shape[1]
    depth = ada_w.shape[0]
    assert depth == 2 and bn == 2, "block structure is specialised to depth 2, batch 2"
    tokens = bn * seq_len
    cw = e_conv_w.shape[2]
    sw = s5_d.shape[1]
    g_dim = sw // SSM_P
    t = T_CHUNK
    nc = seq_len // t
    nc_ctx = ctx_len // t
    assert seq_len % t == 0 and ctx_len % t == 0 and seq_len % GRID_W == 0

    x2d = x.reshape(tokens, d)

    s_rows = jnp.zeros((SUBLANES, d), F32).at[:bn].set(c).at[bn].set(c_ctx)
    mods = _ada_call(s_rows, ada_w, ada_b)

    def lat_mod(layer, k):
        return mods[layer, :bn, k * d:(k + 1) * d].reshape(bn, 1, d)

    li = 0
    w_in = e_w_in[li]
    wv = w_in[:, :cw].astype(BF16)
    wg = w_in[:, cw:2 * cw].astype(BF16)
    wut = w_in[:, 2 * cw:].T.astype(BF16)
    toep, bs, wst, coef = _s5_prep_call(s5_a_re[li], s5_a_im[li], s5_log_step[li], s5_b_re[li],
                                        s5_b_im[li], s5_c_re[li], s5_c_im[li], s5_d[li])

    g_row = norm_g[0, 0].reshape(1, d)
    ctx_shift = mods[0, bn:bn + 1, 0:d]
    ctx_scale = mods[0, bn:bn + 1, d:2 * d]
    uct = _ctx_in_call(ctx.reshape(bn * ctx_len, d), g_row, ctx_shift, ctx_scale, wut)
    uc4 = uct.reshape(g_dim, SSM_P, bn * nc_ctx, t)
    rows_ctx = -(-(bn * nc_ctx) // SUBLANES) * SUBLANES
    uc4 = jnp.pad(uc4, ((0, 0), (0, 0), (0, rows_ctx - bn * nc_ctx), (0, 0)))
    zero_h0 = jnp.zeros((g_dim, SUBLANES, LANES), F32)
    (h0s,) = _s5_call(uc4, bs, coef, zero_h0, nb=bn, nc=nc_ctx)

    tm_in = _pick_tile(seq_len, 512)
    z, ut = _even_in_call(x2d, g_row, lat_mod(0, 0), lat_mod(0, 1), wv, wg, wut, seq_len, tm_in)
    yc = _conv_call(z, e_conv_w[li], e_conv_b[li], e_ln_g[li], e_ln_b[li], seq_len,
                    _pick_tile(seq_len, 256))
    u4 = ut.reshape(g_dim, SSM_P, bn * nc, t)
    y4, _ = _s5_call(u4, bs, coef, h0s, toep, wst, nb=bn, nc=nc)
    yt = y4.reshape(sw, tokens)
    w_out = e_w_out[li]
    h = _even_out_call(yc, yt, s5_glu_w[li].T.astype(BF16), s5_glu_b[li].reshape(sw, 1),
                       w_out[:cw].astype(BF16), w_out[cw:].astype(BF16), x2d, lat_mod(0, 2),
                       seq_len, _pick_tile(seq_len, 512))

    def conv_ffn(h, layer, final_norm):
        tm = _pick_tile(seq_len, 1024)
        tn = 1024
        td = _pick_tile(tm, 256)
        n2 = f_w_in.shape[2]

        def regroup(v):
            lead = v.shape[:-1]
            v = v.reshape(lead + (2, n2 // tn, tn // 2))
            return jnp.swapaxes(v, -3, -2).reshape(lead + (n2,))

        cw = jnp.pad(regroup(f_conv_w[layer].reshape(9, n2)), ((0, 16 - 9), (0, 0)))
        cb = regroup(f_conv_b[layer]).reshape(1, n2)
        act, zb = _ffn_up_call(h, norm_g[layer, 1].reshape(1, d), lat_mod(layer, 3), lat_mod(layer, 4),
                               f_w_in[layer].astype(BF16), cw, cb, seq_len, tm, tn)
        return _ffn_down_call(act, zb, cw, cb, f_w_out[layer].astype(BF16), h, lat_mod(layer, 5),
                              final_g, seq_len, tm, tn, td, final_norm)

    h = conv_ffn(h, 0, False)

    li = 0
    w = o_w_in.shape[2] // 2
    h = _odd_call(h, norm_g[1, 0].reshape(1, d), lat_mod(1, 0), lat_mod(1, 1),
                  o_w_in[li][:, :w].astype(BF16), o_w_in[li][:, w:].astype(BF16),
                  o_ln_g[li].reshape(1, w), o_ln_b[li].reshape(1, w),
                  o_sgu_w[li].astype(BF16), o_sgu_b[li].reshape(SGU_HEADS, SGU_CHUNK, 1),
                  o_w_out[li].astype(BF16), lat_mod(1, 2), seq_len, _pick_tile(seq_len, 256))
    h = conv_ffn(h, 1, True)
    return h.reshape(bn, seq_len, d)
```

```python
import functools
import math

import jax
import jax.numpy as jnp
from jax import lax
from jax.experimental import pallas as pl
from jax.experimental.pallas import tpu as pltpu

F32 = jnp.float32
BF16 = jnp.bfloat16
HIGHEST = lax.Precision.HIGHEST

RMS_EPS = 1e-6
LN_EPS = 1e-5
GRID_W = 64
CONV_K = 31
CONV_HALO = 16
SSM_P = 16
SSM_N = 64
SGU_CHUNK = 128
SGU_HEADS = 8
T_CHUNK = 128
LANES = 128
SUBLANES = 8
MXU_N = 256
VMEM_LIMIT_BYTES = 56 * 1024 * 1024


def _cparams(ngrid):
    return pltpu.CompilerParams(dimension_semantics=("arbitrary",) * ngrid,
                                vmem_limit_bytes=VMEM_LIMIT_BYTES)


def _resident(shape):
    nd = len(shape)
    return pl.BlockSpec(shape, lambda *_: (0,) * nd, pipeline_mode=pl.Buffered(1))


def _sigmoid(x):
    return 1.0 / (1.0 + jnp.exp(-x))


def _silu(x):
    return x * _sigmoid(x)


def _gelu(x):
    return 0.5 * x * (1.0 + jnp.tanh(math.sqrt(2.0 / math.pi) * (x + 0.044715 * (x * x * x))))


def _norm_mod(x, g, shift, scale):
    ms = jnp.mean(x * x, axis=-1, keepdims=True)
    y = x * lax.rsqrt(ms + RMS_EPS) * g
    return y * (1.0 + scale) + shift


def _layer_norm(x, g, b):
    mu = jnp.mean(x, axis=-1, keepdims=True)
    xc = x - mu
    var = jnp.mean(xc * xc, axis=-1, keepdims=True)
    return xc * lax.rsqrt(var + LN_EPS) * g + b


def _ada_body(s_ref, w_ref, b_ref, o_ref):
    s = _silu(s_ref[...])
    o_ref[0] = jnp.dot(s, w_ref[0], preferred_element_type=F32, precision=HIGHEST) + b_ref[0]


def _ada_call(s_rows, ada_w, ada_b):
    depth, d, n = ada_w.shape
    tn = 1024
    return pl.pallas_call(
        _ada_body,
        grid=(depth, n // tn),
        in_specs=[pl.BlockSpec((SUBLANES, d), lambda l, j: (0, 0)),
                  pl.BlockSpec((1, d, tn), lambda l, j: (l, 0, j)),
                  pl.BlockSpec((1, 1, tn), lambda l, j: (l, 0, j))],
        out_specs=pl.BlockSpec((1, SUBLANES, tn), lambda l, j: (l, 0, j)),
        out_shape=jax.ShapeDtypeStruct((depth, SUBLANES, n), F32),
        compiler_params=_cparams(2),
        name="ada_mod",
    )(s_rows, ada_w, ada_b.reshape(depth, 1, n))


def _even_in_body(x_ref, g_ref, sh_ref, sc_ref, wv_ref, wg_ref, wut_ref, z_ref, ut_ref):
    hn = _norm_mod(x_ref[...], g_ref[...], sh_ref[0], sc_ref[0]).astype(BF16)
    av = jnp.dot(hn, wv_ref[...], preferred_element_type=F32)
    ag = jnp.dot(hn, wg_ref[...], preferred_element_type=F32)
    z_ref[...] = av * _sigmoid(ag)
    ut_ref[...] = lax.dot_general(wut_ref[...], hn, (((1,), (1,)), ((), ())),
                                  preferred_element_type=F32)


def _even_in_call(x2d, g, shift, scale, wv, wg, wut, seq_len, tm):
    tokens, d = x2d.shape
    cw = wv.shape[1]
    sw = wut.shape[0]
    tiles_per_seq = seq_len // tm
    mod_spec = pl.BlockSpec((1, 1, d), lambda i: (i // tiles_per_seq, 0, 0))
    return pl.pallas_call(
        _even_in_body,
        grid=(tokens // tm,),
        in_specs=[pl.BlockSpec((tm, d), lambda i: (i, 0)),
                  _resident((1, d)), mod_spec, mod_spec,
                  _resident((d, cw)), _resident((d, cw)), _resident((sw, d))],
        out_specs=[pl.BlockSpec((tm, cw), lambda i: (i, 0)),
                   pl.BlockSpec((sw, tm), lambda i: (0, i))],
        out_shape=[jax.ShapeDtypeStruct((tokens, cw), F32),
                   jax.ShapeDtypeStruct((sw, tokens), F32)],
        compiler_params=_cparams(1),
        name="even_in_proj",
    )(x2d, g, shift, scale, wv, wg, wut)


def _ctx_in_body(x_ref, g_ref, sh_ref, sc_ref, wut_ref, ut_ref):
    hn = _norm_mod(x_ref[...], g_ref[...], sh_ref[...], sc_ref[...]).astype(BF16)
    ut_ref[...] = lax.dot_general(wut_ref[...], hn, (((1,), (1,)), ((), ())),
                                  preferred_element_type=F32)


def _ctx_in_call(c2d, g, shift, scale, wut):
    tokens, d = c2d.shape
    sw = wut.shape[0]
    return pl.pallas_call(
        _ctx_in_body,
        out_shape=jax.ShapeDtypeStruct((sw, tokens), F32),
        compiler_params=pltpu.CompilerParams(vmem_limit_bytes=VMEM_LIMIT_BYTES),
        name="ctx_in_proj",
    )(c2d, g, shift, scale, wut)


CONV_ROWS = 128


def _conv_body(zp_ref, zm_ref, zn_ref, w_ref, b_ref, g_ref, beta_ref, o_ref, buf_ref, acc_ref,
               *, tl, tiles_per_seq):
    i = pl.program_id(0)
    first = (i % tiles_per_seq) == 0
    last = (i % tiles_per_seq) == tiles_per_seq - 1
    buf_ref[0:CONV_HALO, :] = jnp.where(first, 0.0, zp_ref[...])
    buf_ref[CONV_HALO:CONV_HALO + tl, :] = zm_ref[...]
    buf_ref[CONV_HALO + tl:2 * CONV_HALO + tl, :] = jnp.where(last, 0.0, zn_ref[...])

    cw = o_ref.shape[1]
    n_rb = tl // CONV_ROWS
    n_cb = cw // LANES
    rows = CONV_ROWS + SUBLANES

    def lane_block(r0, c0):
        acc = jnp.zeros((CONV_ROWS, LANES), F32)
        for r in range(SUBLANES):
            part = None
            for a in range(4):
                k = SUBLANES * a + r - 1
                if k < 0 or k >= CONV_K:
                    continue
                term = buf_ref[pl.ds(r0 + SUBLANES * a, rows), c0:c0 + LANES] * \
                    w_ref[k:k + 1, c0:c0 + LANES]
                part = term if part is None else part + term
            if r:
                part = pltpu.roll(part, rows - r, axis=0)
            acc = acc + part[:CONV_ROWS]
        acc_ref[pl.ds(r0, CONV_ROWS), c0:c0 + LANES] = acc + b_ref[:, c0:c0 + LANES]

    def block(rb, carry):
        r0 = pl.multiple_of(rb * CONV_ROWS, CONV_ROWS)
        for cb in range(n_cb):
            lane_block(r0, cb * LANES)
        return carry

    lax.fori_loop(0, n_rb, block, 0)

    ln_rows = 128

    def ln_block(rb, carry):
        r0 = pl.multiple_of(rb * ln_rows, ln_rows)
        y = _layer_norm(acc_ref[pl.ds(r0, ln_rows), :], g_ref[...], beta_ref[...])
        o_ref[pl.ds(r0, ln_rows), :] = _silu(y).astype(o_ref.dtype)
        return carry

    lax.fori_loop(0, tl // ln_rows, ln_block, 0)


def _conv_call(z2d, conv_w, conv_b, ln_g, ln_b, seq_len, tl):
    tokens, cw = z2d.shape
    tiles_per_seq = seq_len // tl
    hb = tl // CONV_HALO
    n_hblocks = tokens // CONV_HALO
    w_pad = jnp.pad(conv_w, ((0, 32 - CONV_K), (0, 0)))
    body = functools.partial(_conv_body, tl=tl, tiles_per_seq=tiles_per_seq)
    return pl.pallas_call(
        body,
        grid=(tokens // tl,),
        in_specs=[pl.BlockSpec((CONV_HALO, cw), lambda i: (jnp.maximum(i * hb - 1, 0), 0)),
                  pl.BlockSpec((tl, cw), lambda i: (i, 0)),
                  pl.BlockSpec((CONV_HALO, cw), lambda i: (jnp.minimum((i + 1) * hb, n_hblocks - 1), 0)),
                  _resident((32, cw)), _resident((1, cw)), _resident((1, cw)), _resident((1, cw))],
        out_specs=pl.BlockSpec((tl, cw), lambda i: (i, 0)),
        out_shape=jax.ShapeDtypeStruct((tokens, cw), BF16),
        scratch_shapes=[pltpu.VMEM((tl + 2 * CONV_HALO, cw), F32), pltpu.VMEM((tl, cw), F32)],
        compiler_params=_cparams(1),
        name="conformer_conv",
    )(z2d, z2d, z2d, w_pad, conv_b.reshape(1, cw), ln_g.reshape(1, cw), ln_b.reshape(1, cw))


def _s5_prep_body(ar_row_ref, ai_row_ref, ar_col_ref, ai_col_ref, ls_ref,
                  bt_re_ref, bt_im_ref, c_re_ref, c_im_ref, ct_re_ref, ct_im_ref, dpq_ref,
                  toep_ref, bs_ref, wst_ref, coef_ref, kv_ref):
    t = T_CHUNK
    n = SSM_N
    p_dim = SSM_P
    lane2 = lax.broadcasted_iota(jnp.int32, (1, 2 * t), 1)
    lane1 = lax.broadcasted_iota(jnp.int32, (1, t), 1).astype(F32)
    sub1 = lax.broadcasted_iota(jnp.int32, (t, 1), 0).astype(F32)

    bs_ref[...] = jnp.zeros(bs_ref.shape, bs_ref.dtype)
    wst_ref[...] = jnp.zeros(wst_ref.shape, wst_ref.dtype)
    coef_ref[...] = jnp.zeros(coef_ref.shape, coef_ref.dtype)

    kv = dpq_ref[0] * (lane2 == t).astype(F32)
    for d in range(2):
        dt = jnp.exp(ls_ref[0, d])
        ar = ar_row_ref[0, d]
        ai = ai_row_ref[0, d]
        lam_row = ar * dt
        ang_row = ai * dt
        lam_col = ar_col_ref[0, d] * dt
        ang_col = ai_col_ref[0, d] * dt
        mag = jnp.exp(lam_row)
        ab_re = mag * jnp.cos(ang_row)
        ab_im = mag * jnp.sin(ang_row)
        den = ar * ar + ai * ai
        num_re = ab_re - 1.0
        f_re = (num_re * ar + ab_im * ai) / den
        f_im = (ab_im * ar - num_re * ai) / den
        bbt_re = f_re * bt_re_ref[0, d] - f_im * bt_im_ref[0, d]
        bbt_im = f_re * bt_im_ref[0, d] + f_im * bt_re_ref[0, d]

        if d == 0:
            lag2 = jnp.maximum(lane2 - t, 0).astype(F32)
            msk2 = (lane2 >= t).astype(F32)
        else:
            lag2 = jnp.maximum(t - lane2, 0).astype(F32)
            msk2 = (lane2 <= t).astype(F32)
        pm = jnp.exp(lam_col * lag2) * msk2
        pt_re = pm * jnp.cos(ang_col * lag2)
        pt_im = pm * jnp.sin(ang_col * lag2)
        cb_re = jnp.concatenate(
            [c_re_ref[0, d, pl.ds(p, 1), :] * bbt_re - c_im_ref[0, d, pl.ds(p, 1), :] * bbt_im
             for p in range(p_dim)], axis=0)
        cb_im = jnp.concatenate(
            [c_re_ref[0, d, pl.ds(p, 1), :] * bbt_im + c_im_ref[0, d, pl.ds(p, 1), :] * bbt_re
             for p in range(p_dim)], axis=0)
        kv = kv + (jnp.dot(cb_re, pt_re, preferred_element_type=F32, precision=HIGHEST)
                   - jnp.dot(cb_im, pt_im, preferred_element_type=F32, precision=HIGHEST))

        lag_col = (t - 1.0 - sub1) if d == 0 else sub1
        pw_m = jnp.exp(lag_col * lam_row)
        pw_re = pw_m * jnp.cos(lag_col * ang_row)
        pw_im = pw_m * jnp.sin(lag_col * ang_row)
        for q in range(p_dim):
            br = bbt_re[q:q + 1, :]
            bi = bbt_im[q:q + 1, :]
            bs_ref[0, q * t:(q + 1) * t, (2 * d) * LANES:(2 * d) * LANES + n] = \
                (pw_re * br - pw_im * bi).astype(bs_ref.dtype)
            bs_ref[0, q * t:(q + 1) * t, (2 * d + 1) * LANES:(2 * d + 1) * LANES + n] = \
                (pw_re * bi + pw_im * br).astype(bs_ref.dtype)

        lag_row = (lane1 + 1.0) if d == 0 else (t - lane1)
        qm = jnp.exp(lam_col * lag_row)
        q_re = qm * jnp.cos(ang_col * lag_row)
        q_im = qm * jnp.sin(ang_col * lag_row)
        for p in range(p_dim):
            cr = ct_re_ref[0, d, :, p:p + 1]
            ci = ct_im_ref[0, d, :, p:p + 1]
            wst_ref[0, (2 * d) * LANES:(2 * d) * LANES + n, p * t:(p + 1) * t] = \
                (cr * q_re - ci * q_im).astype(wst_ref.dtype)
            wst_ref[0, (2 * d + 1) * LANES:(2 * d + 1) * LANES + n, p * t:(p + 1) * t] = \
                (-(cr * q_im + ci * q_re)).astype(wst_ref.dtype)

        tt = float(t)
        at_m = jnp.exp(lam_row * tt)
        coef_ref[0, 2 * d:2 * d + 1, 0:n] = at_m * jnp.cos(ang_row * tt)
        coef_ref[0, 2 * d + 1:2 * d + 2, 0:n] = at_m * jnp.sin(ang_row * tt)

    kv_ref[...] = kv

    def toep_block(p, carry):
        kv_p = kv_ref[pl.ds(pl.multiple_of(p * p_dim, p_dim), p_dim), :]
        for q in range(p_dim):
            row = jnp.broadcast_to(kv_p[q:q + 1, :], (t, 2 * t))
            rolled = pltpu.roll(row, 0, axis=1, stride=1, stride_axis=0)
            toep_ref[0, q * t:(q + 1) * t, pl.ds(pl.multiple_of(p * t, t), t)] = \
                rolled[:, t:].astype(toep_ref.dtype)
        return carry

    lax.fori_loop(0, p_dim, toep_block, 0)


def _s5_prep_call(a_re, a_im, log_step, b_re, b_im, c_re, c_im, d_skip):
    g_dim = a_re.shape[1]
    n, p_dim, t = SSM_N, SSM_P, T_CHUNK
    gm = lambda v: jnp.swapaxes(v, 0, 1)
    ar, ai = gm(a_re), gm(a_im)
    dpq = (jnp.eye(p_dim, dtype=F32)[None] * d_skip.reshape(g_dim, p_dim, 1)).reshape(g_dim, p_dim * p_dim, 1)
    args = (ar[:, :, None, :], ai[:, :, None, :], ar[:, :, :, None], ai[:, :, :, None],
            gm(log_step)[:, :, None, None],
            jnp.swapaxes(gm(b_re), 2, 3), jnp.swapaxes(gm(b_im), 2, 3),
            gm(c_re), gm(c_im), jnp.swapaxes(gm(c_re), 2, 3), jnp.swapaxes(gm(c_im), 2, 3), dpq)

    def spec(v):
        blk = (1,) + v.shape[1:]
        nd = v.ndim
        return pl.BlockSpec(blk, lambda g: (g,) + (0,) * (nd - 1))

    out_shapes = [jax.ShapeDtypeStruct((g_dim, p_dim * t, p_dim * t), BF16),
                  jax.ShapeDtypeStruct((g_dim, p_dim * t, 4 * LANES), BF16),
                  jax.ShapeDtypeStruct((g_dim, 4 * LANES, p_dim * t), BF16),
                  jax.ShapeDtypeStruct((g_dim, SUBLANES, LANES), F32)]
    return pl.pallas_call(
        _s5_prep_body,
        grid=(g_dim,),
        in_specs=[spec(v) for v in args],
        out_specs=[spec(s) for s in out_shapes],
        out_shape=out_shapes,
        scratch_shapes=[pltpu.VMEM((p_dim * p_dim, 2 * t), F32)],
        compiler_params=_cparams(1),
        name="s5_prep",
    )(*args)


def _s5_body(*refs, nb, nc, with_y):
    if with_y:
        u_ref, bs_ref, coef_ref, h0_ref, toep_ref, wst_ref, y_ref, fin_ref, s_scr, h_scr = refs
    else:
        u_ref, bs_ref, coef_ref, h0_ref, fin_ref, s_scr = refs
    p_dim = u_ref.shape[1]
    t = T_CHUNK
    u = jnp.concatenate([u_ref[0, q] for q in range(p_dim)], axis=1).astype(BF16)
    s_scr[...] = jnp.dot(u, bs_ref[0], preferred_element_type=F32)
    if with_y:
        h_scr[...] = jnp.zeros(h_scr.shape, h_scr.dtype)

    af_re, af_im = coef_ref[0, 0:1, :], coef_ref[0, 1:2, :]
    ab_re, ab_im = coef_ref[0, 2:3, :], coef_ref[0, 3:4, :]
    init = tuple(h0_ref[0, r:r + 1, :] for r in range(4 * nb))

    blk = SUBLANES if nc % SUBLANES == 0 else nc

    def block(k, carry):
        new = list(carry)
        for b in range(nb):
            f0 = b * nc + blk * k
            b0 = b * nc + nc - blk - blk * k
            if blk == SUBLANES:
                f0 = pl.multiple_of(f0, SUBLANES)
                b0 = pl.multiple_of(b0, SUBLANES)
            sf_re = s_scr[pl.ds(f0, blk), 0:LANES]
            sf_im = s_scr[pl.ds(f0, blk), LANES:2 * LANES]
            sb_re = s_scr[pl.ds(b0, blk), 2 * LANES:3 * LANES]
            sb_im = s_scr[pl.ds(b0, blk), 3 * LANES:4 * LANES]
            fr, fi, br, bi = new[4 * b:4 * b + 4]
            hf_re, hf_im, hb_re, hb_im = [], [], [], []
            for j in range(blk):
                jb = blk - 1 - j
                hf_re.append(fr)
                hf_im.append(fi)
                hb_re.insert(0, br)
                hb_im.insert(0, bi)
                fr, fi = (af_re * fr - af_im * fi + sf_re[j:j + 1],
                          af_re * fi + af_im * fr + sf_im[j:j + 1])
                br, bi = (ab_re * br - ab_im * bi + sb_re[jb:jb + 1],
                          ab_re * bi + ab_im * br + sb_im[jb:jb + 1])
            if with_y:
                h_scr[pl.ds(f0, blk), 0:LANES] = jnp.concatenate(hf_re, axis=0)
                h_scr[pl.ds(f0, blk), LANES:2 * LANES] = jnp.concatenate(hf_im, axis=0)
                h_scr[pl.ds(b0, blk), 2 * LANES:3 * LANES] = jnp.concatenate(hb_re, axis=0)
                h_scr[pl.ds(b0, blk), 3 * LANES:4 * LANES] = jnp.concatenate(hb_im, axis=0)
            new[4 * b:4 * b + 4] = [fr, fi, br, bi]
        return tuple(new)

    if nc == blk:
        fin = block(0, init)
    else:
        fin = lax.fori_loop(0, nc // blk, block, init)
    fin_ref[0] = jnp.concatenate(list(fin) + [jnp.zeros((1, LANES), F32)] * (SUBLANES - 4 * nb), axis=0) \
        if 4 * nb < SUBLANES else jnp.concatenate(list(fin), axis=0)

    if with_y:
        y = jnp.dot(u, toep_ref[0], preferred_element_type=F32)
        y = y + jnp.dot(h_scr[...].astype(BF16), wst_ref[0], preferred_element_type=F32)
        y = _gelu(y)
        for p in range(p_dim):
            y_ref[0, p] = y[:, p * t:(p + 1) * t]


def _s5_call(u4, bs, coef, h0, toep=None, wst=None, *, nb, nc):
    g_dim, p_dim, rows, t = u4.shape
    with_y = toep is not None
    body = functools.partial(_s5_body, nb=nb, nc=nc, with_y=with_y)

    def spec(v):
        blk = (1,) + v.shape[1:]
        nd = v.ndim
        return pl.BlockSpec(blk, lambda g: (g,) + (0,) * (nd - 1))

    fin_shape = jax.ShapeDtypeStruct((g_dim, SUBLANES, LANES), F32)
    args = [u4, bs, coef, h0]
    scratch = [pltpu.VMEM((rows, 4 * LANES), F32)]
    if with_y:
        args += [toep, wst]
        out_shape = [jax.ShapeDtypeStruct(u4.shape, F32), fin_shape]
        scratch.append(pltpu.VMEM((rows, 4 * LANES), F32))
    else:
        out_shape = [fin_shape]
    outs = pl.pallas_call(
        body,
        grid=(g_dim,),
        in_specs=[spec(v) for v in args],
        out_specs=[spec(s) for s in out_shape],
        out_shape=out_shape,
        scratch_shapes=scratch,
        compiler_params=_cparams(1),
        name="s5_chunks" if with_y else "s5_ctx_states",
    )(*args)
    return outs


def _even_out_body(yc_ref, yt_ref, gwt_ref, gb_ref, woa_ref, wob_ref, x_ref, g1_ref, o_ref):
    yt = yt_ref[...]
    gl = jnp.dot(gwt_ref[...], yt.astype(BF16), preferred_element_type=F32) + gb_ref[...]
    ys = (yt * _sigmoid(gl)).astype(BF16)
    mix = jnp.dot(yc_ref[...], woa_ref[...], preferred_element_type=F32)
    mix = mix + jnp.dot(ys.T, wob_ref[...], preferred_element_type=F32)
    o_ref[...] = x_ref[...] + g1_ref[0] * mix


def _even_out_call(yc, yt, gwt, gb, woa, wob, x2d, g1, seq_len, tm):
    tokens, d = x2d.shape
    cw = yc.shape[1]
    sw = yt.shape[0]
    tiles_per_seq = seq_len // tm
    return pl.pallas_call(
        _even_out_body,
        grid=(tokens // tm,),
        in_specs=[pl.BlockSpec((tm, cw), lambda i: (i, 0)),
                  pl.BlockSpec((sw, tm), lambda i: (0, i)),
                  _resident((sw, sw)), _resident((sw, 1)),
                  _resident((cw, d)), _resident((sw, d)),
                  pl.BlockSpec((tm, d), lambda i: (i, 0)),
                  pl.BlockSpec((1, 1, d), lambda i: (i // tiles_per_seq, 0, 0))],
        out_specs=pl.BlockSpec((tm, d), lambda i: (i, 0)),
        out_shape=jax.ShapeDtypeStruct((tokens, d), F32),
        compiler_params=_cparams(1),
        name="even_out_proj",
    )(yc, yt, gwt, gb, woa, wob, x2d, g1)


def _odd_body(x_ref, g_ref, sh_ref, sc_ref, wu_ref, wv_ref, lg_ref, lb_ref, sw_ref, sb_ref,
              wo_ref, g1_ref, o_ref, gated_ref):
    x = x_ref[...]
    tm = x.shape[0]
    hn = _norm_mod(x, g_ref[...], sh_ref[0], sc_ref[0]).astype(BF16)
    v = _gelu(jnp.dot(hn, wv_ref[...], preferred_element_type=F32))
    v = _layer_norm(v, lg_ref[...], lb_ref[...]).astype(BF16)
    u = _gelu(jnp.dot(hn, wu_ref[...], preferred_element_type=F32))
    hd = v.shape[1] // SGU_HEADS
    for ch in range(tm // SGU_CHUNK):
        r0 = ch * SGU_CHUNK
        for h in range(SGU_HEADS):
            s = jnp.dot(sw_ref[h], v[r0:r0 + SGU_CHUNK, h * hd:(h + 1) * hd],
                        preferred_element_type=F32) + sb_ref[h]
            gated_ref[r0:r0 + SGU_CHUNK, h * hd:(h + 1) * hd] = \
                (u[r0:r0 + SGU_CHUNK, h * hd:(h + 1) * hd] * s).astype(BF16)
    mix = jnp.dot(gated_ref[...], wo_ref[...], preferred_element_type=F32)
    o_ref[...] = x + g1_ref[0] * mix


def _odd_call(x2d, g, shift, scale, wu, wv, ln_g, ln_b, sgu_w, sgu_b, wo, g1, seq_len, tm):
    tokens, d = x2d.shape
    w = wu.shape[1]
    tiles_per_seq = seq_len // tm
    mod_spec = pl.BlockSpec((1, 1, d), lambda i: (i // tiles_per_seq, 0, 0))
    return pl.pallas_call(
        _odd_body,
        grid=(tokens // tm,),
        in_specs=[pl.BlockSpec((tm, d), lambda i: (i, 0)),
                  _resident((1, d)), mod_spec, mod_spec,
                  _resident((d, w)), _resident((d, w)), _resident((1, w)), _resident((1, w)),
                  _resident(sgu_w.shape), _resident(sgu_b.shape),
                  _resident((w, d)), mod_spec],
        out_specs=pl.BlockSpec((tm, d), lambda i: (i, 0)),
        out_shape=jax.ShapeDtypeStruct((tokens, d), F32),
        scratch_shapes=[pltpu.VMEM((tm, w), BF16)],
        compiler_params=_cparams(1),
        name="odd_gmlp",
    )(x2d, g, shift, scale, wu, wv, ln_g, ln_b, sgu_w, sgu_b, wo, g1)


def _row_from_left(v):
    n_vr = GRID_W // SUBLANES
    first = lax.broadcasted_iota(jnp.int32, (SUBLANES, LANES), 0) == 0
    r = [pltpu.roll(v[SUBLANES * k:SUBLANES * (k + 1)], 1, axis=0) for k in range(n_vr)]
    return jnp.concatenate([jnp.where(first, r[k - 1] if k else 0.0, r[k]) for k in range(n_vr)], axis=0)


def _row_from_right(v):
    n_vr = GRID_W // SUBLANES
    last = lax.broadcasted_iota(jnp.int32, (SUBLANES, LANES), 0) == SUBLANES - 1
    r = [pltpu.roll(v[SUBLANES * k:SUBLANES * (k + 1)], SUBLANES - 1, axis=0) for k in range(n_vr)]
    return jnp.concatenate([jnp.where(last, r[k + 1] if k < n_vr - 1 else 0.0, r[k])
                            for k in range(n_vr)], axis=0)


def _conv3x3_row(up, mid, dn, cw_ref, cb_ref, c0):
    w = lambda k: cw_ref[pl.ds(k, 1), pl.ds(c0, LANES)]
    left = up * w(0) + mid * w(3) + dn * w(6)
    cent = up * w(1) + mid * w(4) + dn * w(7)
    right = up * w(2) + mid * w(5) + dn * w(8)
    return cent + _row_from_left(left) + _row_from_right(right) + cb_ref[:, pl.ds(c0, LANES)]


def _ffn_up_body(x_ref, g_ref, sh_ref, sc_ref, wa_ref, wg_ref, cw_ref, cb_ref, act_ref, zb_ref,
                 hn_ref, z_ref):
    @pl.when(pl.program_id(1) == 0)
    def _():
        hn_ref[...] = _norm_mod(x_ref[...], g_ref[...], sh_ref[0], sc_ref[0]).astype(BF16)

    tm, tn = z_ref.shape
    half = tn // 2
    n_rows = tm // GRID_W
    n_cb = half // LANES
    n_groups = half // MXU_N
    m_blocks = tm // MXU_N

    def group_dots(gi):
        out = []
        for w_ref, base in ((wa_ref, 0), (wg_ref, half)):
            for mb in range(m_blocks):
                out.append((w_ref, base, mb))
        return [functools.partial(block_dot, w_ref, base, gi, mb) for w_ref, base, mb in out]

    def block_dot(w_ref, base, gi, mb):
        rows = slice(mb * MXU_N, (mb + 1) * MXU_N)
        cols = slice(gi * MXU_N, (gi + 1) * MXU_N)
        z_ref[rows, base + gi * MXU_N:base + (gi + 1) * MXU_N] = jnp.dot(
            hn_ref[rows, :], w_ref[:, cols], preferred_element_type=F32)

    def conv_item(r, cb):
        def conv(c0):
            rows = [z_ref[(r + dy) * GRID_W:(r + dy + 1) * GRID_W, c0:c0 + LANES] for dy in (-1, 0, 1)]
            return _conv3x3_row(*rows, cw_ref, cb_ref, c0)

        a = conv(cb * LANES)
        g = conv(half + cb * LANES)
        act_ref[r * GRID_W:(r + 1) * GRID_W, cb * LANES:(cb + 1) * LANES] = (a * _silu(g)).astype(act_ref.dtype)

    act_ref[0:GRID_W, :] = jnp.zeros((GRID_W, half), act_ref.dtype)
    act_ref[tm - GRID_W:tm, :] = jnp.zeros((GRID_W, half), act_ref.dtype)

    for d in group_dots(0):
        d()
    lanes_per_group = MXU_N // LANES
    for gi in range(n_groups):
        items = [(r, gi * lanes_per_group + s) for r in range(1, n_rows - 1) for s in range(lanes_per_group)]
        dots = group_dots(gi + 1) if gi + 1 < n_groups else []
        every = -(-len(items) // (len(dots) + 1))
        for k, (r, cb) in enumerate(items):
            conv_item(r, cb)
            if dots and (k + 1) % every == 0:
                dots.pop(0)()
        for d in dots:
            d()

    for e, r in enumerate((0, 1, n_rows - 2, n_rows - 1)):
        zb_ref[0, e] = z_ref[r * GRID_W:(r + 1) * GRID_W, :].astype(zb_ref.dtype)


def _ffn_up_call(x2d, g, shift, scale, w, cw, cb, seq_len, tm, tn):
    tokens, d = x2d.shape
    n2 = w.shape[1]
    half = tn // 2
    gate_blk0 = (n2 // 2) // half
    tiles_per_seq = seq_len // tm
    mod_spec = pl.BlockSpec((1, 1, d), lambda i, j: (i // tiles_per_seq, 0, 0))
    return pl.pallas_call(
        _ffn_up_body,
        grid=(tokens // tm, n2 // tn),
        in_specs=[pl.BlockSpec((tm, d), lambda i, j: (i, 0)),
                  _resident((1, d)), mod_spec, mod_spec,
                  pl.BlockSpec((d, half), lambda i, j: (0, j)),
                  pl.BlockSpec((d, half), lambda i, j: (0, gate_blk0 + j)),
                  pl.BlockSpec((16, tn), lambda i, j: (0, j)),
                  pl.BlockSpec((1, tn), lambda i, j: (0, j))],
        out_specs=[pl.BlockSpec((tm, half), lambda i, j: (i, j)),
                   pl.BlockSpec((1, 4, GRID_W, tn), lambda i, j: (i, 0, 0, j))],
        out_shape=[jax.ShapeDtypeStruct((tokens, n2 // 2), BF16),
                   jax.ShapeDtypeStruct((tokens // tm, 4, GRID_W, n2), BF16)],
        scratch_shapes=[pltpu.VMEM((tm, d), BF16), pltpu.VMEM((tm, tn), F32)],
        compiler_params=_cparams(2),
        name="ffn_up_conv",
    )(x2d, g, shift, scale, w, w, cw, cb)


def _ffn_down_body(act_ref, zc_ref, zp_ref, zn_ref, cw_ref, cb_ref, wo_ref, x_ref, g2_ref, fg_ref, o_ref,
                   acts_ref, *, td, tn, down_per_up, up_per_seq, final_norm):
    k = pl.program_id(0)
    up_tile = k // down_per_up
    pos = k % down_per_up
    top_ok = jnp.where((up_tile % up_per_seq) == 0, 0.0, 1.0)
    bot_ok = jnp.where((up_tile % up_per_seq) == up_per_seq - 1, 0.0, 1.0)
    dff = acts_ref.shape[1]
    half = tn // 2
    acts_ref[...] = act_ref[...]

    def edge_row(rows, dst):
        def blk(j, carry):
            for s in range(half // LANES):
                cz = pl.multiple_of(j * tn + s * LANES, LANES)
                cg = pl.multiple_of(cz + half, LANES)
                a = _conv3x3_row(*rows(cz), cw_ref, cb_ref, cz)
                g = _conv3x3_row(*rows(cg), cw_ref, cb_ref, cg)
                acts_ref[dst:dst + GRID_W, pl.ds(pl.multiple_of(j * half + s * LANES, LANES), LANES)] = \
                    (a * _silu(g)).astype(acts_ref.dtype)
            return carry

        lax.fori_loop(0, 2 * dff // tn, blk, 0)

    @pl.when(pos == 0)
    def _():
        edge_row(lambda c: (zp_ref[0, 0, :, pl.ds(c, LANES)].astype(F32) * top_ok,
                            zc_ref[0, 0, :, pl.ds(c, LANES)].astype(F32),
                            zc_ref[0, 1, :, pl.ds(c, LANES)].astype(F32)), 0)

    @pl.when(pos == down_per_up - 1)
    def _():
        edge_row(lambda c: (zc_ref[0, 2, :, pl.ds(c, LANES)].astype(F32),
                            zc_ref[0, 3, :, pl.ds(c, LANES)].astype(F32),
                            zn_ref[0, 0, :, pl.ds(c, LANES)].astype(F32) * bot_ok), td - GRID_W)

    y = x_ref[...] + g2_ref[0] * jnp.dot(acts_ref[...], wo_ref[...], preferred_element_type=F32)
    if final_norm:
        ms = jnp.mean(y * y, axis=-1, keepdims=True)
        y = y * lax.rsqrt(ms + RMS_EPS) * fg_ref[...]
    o_ref[...] = y


def _ffn_down_call(act, zb, cw, cb, wo, x2d, g2, final_g, seq_len, tm, tn, td, final_norm):
    tokens, d = x2d.shape
    dff = act.shape[1]
    n2 = zb.shape[3]
    n_up = zb.shape[0]
    down_per_up = tm // td
    tiles_per_seq = seq_len // td
    body = functools.partial(_ffn_down_body, td=td, tn=tn, down_per_up=down_per_up,
                             up_per_seq=seq_len // tm, final_norm=final_norm)
    return pl.pallas_call(
        body,
        grid=(tokens // td,),
        in_specs=[pl.BlockSpec((td, dff), lambda k: (k, 0)),
                  pl.BlockSpec((1, 4, GRID_W, n2), lambda k: (k // down_per_up, 0, 0, 0)),
                  pl.BlockSpec((1, 1, GRID_W, n2),
                               lambda k: (jnp.maximum(k // down_per_up - 1, 0), 3, 0, 0)),
                  pl.BlockSpec((1, 1, GRID_W, n2),
                               lambda k: (jnp.minimum(k // down_per_up + 1, n_up - 1), 0, 0, 0)),
                  _resident((16, n2)), _resident((1, n2)), _resident((dff, d)),
                  pl.BlockSpec((td, d), lambda k: (k, 0)),
                  pl.BlockSpec((1, 1, d), lambda k: (k // tiles_per_seq, 0, 0)),
                  _resident((1, d))],
        out_specs=pl.BlockSpec((td, d), lambda k: (k, 0)),
        out_shape=jax.ShapeDtypeStruct((tokens, d), F32),
        scratch_shapes=[pltpu.VMEM((td, dff), BF16)],
        compiler_params=_cparams(1),
        name="ffn_edge_down",
    )(act, zb, zb, zb, cw, cb, wo, x2d, g2, final_g.reshape(1, d))


def _pick_tile(seq_len, want):
    tm = min(want, seq_len)
    assert seq_len % tm == 0
    return tm


def kernel(x, c, ctx, c_ctx, ada_w, ada_b, norm_g, final_g, e_w_in, e_conv_w, e_conv_b, e_ln_g, e_ln_b, s5_a_re, s5_a_im, s5_log_step, s5_b_re, s5_b_im, s5_c_re, s5_c_im, s5_d, s5_glu_w, s5_glu_b, e_w_out, o_w_in, o_ln_g, o_ln_b, o_sgu_w, o_sgu_b, o_w_out, f_w_in, f_conv_w, f_conv_b, f_w_out):
    bn, seq_len, d = x.shape
    ctx_len = ctx.shape[1]
    depth = ada_w.shape[0]
    assert depth == 2 and bn == 2, "block structure is specialised to depth 2, batch 2"
    tokens = bn * seq_len
    cw = e_conv_w.shape[2]
    sw = s5_d.shape[1]
    g_dim = sw // SSM_P
    t = T_CHUNK
    nc = seq_len // t
    nc_ctx = ctx_len // t
    assert seq_len % t == 0 and ctx_len % t == 0 and seq_len % GRID_W == 0

    x2d = x.reshape(tokens, d)

    s_rows = jnp.zeros((SUBLANES, d), F32).at[:bn].set(c).at[bn].set(c_ctx)
    mods = _ada_call(s_rows, ada_w, ada_b)

    def lat_mod(layer, k):
        return mods[layer, :bn, k * d:(k + 1) * d].reshape(bn, 1, d)

    li = 0
    w_in = e_w_in[li]
    wv = w_in[:, :cw].astype(BF16)
    wg = w_in[:, cw:2 * cw].astype(BF16)
    wut = w_in[:, 2 * cw:].T.astype(BF16)
    toep, bs, wst, coef = _s5_prep_call(s5_a_re[li], s5_a_im[li], s5_log_step[li], s5_b_re[li],
                                        s5_b_im[li], s5_c_re[li], s5_c_im[li], s5_d[li])

    g_row = norm_g[0, 0].reshape(1, d)
    ctx_shift = mods[0, bn:bn + 1, 0:d]
    ctx_scale = mods[0, bn:bn + 1, d:2 * d]
    uct = _ctx_in_call(ctx.reshape(bn * ctx_len, d), g_row, ctx_shift, ctx_scale, wut)
    uc4 = uct.reshape(g_dim, SSM_P, bn * nc_ctx, t)
    rows_ctx = -(-(bn * nc_ctx) // SUBLANES) * SUBLANES
    uc4 = jnp.pad(uc4, ((0, 0), (0, 0), (0, rows_ctx - bn * nc_ctx), (0, 0)))
    zero_h0 = jnp.zeros((g_dim, SUBLANES, LANES), F32)
    (h0s,) = _s5_call(uc4, bs, coef, zero_h0, nb=bn, nc=nc_ctx)

    tm_in = _pick_tile(seq_len, 512)
    z, ut = _even_in_call(x2d, g_row, lat_mod(0, 0), lat_mod(0, 1), wv, wg, wut, seq_len, tm_in)
    yc = _conv_call(z, e_conv_w[li], e_conv_b[li], e_ln_g[li], e_ln_b[li], seq_len,
                    _pick_tile(seq_len, 256))
    u4 = ut.reshape(g_dim, SSM_P, bn * nc, t)
    y4, _ = _s5_call(u4, bs, coef, h0s, toep, wst, nb=bn, nc=nc)
    yt = y4.reshape(sw, tokens)
    w_out = e_w_out[li]
    h = _even_out_call(yc, yt, s5_glu_w[li].T.astype(BF16), s5_glu_b[li].reshape(sw, 1),
                       w_out[:cw].astype(BF16), w_out[cw:].astype(BF16), x2d, lat_mod(0, 2),
                       seq_len, _pick_tile(seq_len, 512))

    def conv_ffn(h, layer, final_norm):
        tm = _pick_tile(seq_len, 1024)
        tn = 1024
        td = _pick_tile(tm, 256)
        n2 = f_w_in.shape[2]

        def regroup(v):
            lead = v.shape[:-1]
            v = v.reshape(lead + (2, n2 // tn, tn // 2))
            return jnp.swapaxes(v, -3, -2).reshape(lead + (n2,))

        cw = jnp.pad(regroup(f_conv_w[layer].reshape(9, n2)), ((0, 16 - 9), (0, 0)))
        cb = regroup(f_conv_b[layer]).reshape(1, n2)
        act, zb = _ffn_up_call(h, norm_g[layer, 1].reshape(1, d), lat_mod(layer, 3), lat_mod(layer, 4),
                               f_w_in[layer].astype(BF16), cw, cb, seq_len, tm, tn)
        return _ffn_down_call(act, zb, cw, cb, f_w_out[layer].astype(BF16), h, lat_mod(layer, 5),
                              final_g, seq_len, tm, tn, td, final_norm)

    h = conv_ffn(h, 0, False)

    li = 0
    w = o_w_in.shape[2] // 2
    h = _odd_call(h, norm_g[1, 0].reshape(1, d), lat_mod(1, 0), lat_mod(1, 1),
                  o_w_in[li][:, :w].astype(BF16), o_w_in[li][:, w:].astype(BF16),
                  o_ln_g[li].reshape(1, w), o_ln_b[li].reshape(1, w),
                  o_sgu_w[li].astype(BF16), o_sgu_b[li].reshape(SGU_HEADS, SGU_CHUNK, 1),
                  o_w_out[li].astype(BF16), lat_mod(1, 2), seq_len, _pick_tile(seq_len, 256))
    h = conv_ffn(h, 1, True)
    return h.reshape(bn, seq_len, d)
```

```python
import functools
import math

import jax
import jax.numpy as jnp
from jax import lax
from jax.experimental import pallas as pl
from jax.experimental.pallas import tpu as pltpu

F32 = jnp.float32
BF16 = jnp.bfloat16
HIGHEST = lax.Precision.HIGHEST

RMS_EPS = 1e-6
LN_EPS = 1e-5
GRID_W = 64
CONV_K = 31
CONV_HALO = 16
SSM_P = 16
SSM_N = 64
SGU_CHUNK = 128
SGU_HEADS = 8
T_CHUNK = 128
LANES = 128
SUBLANES = 8
VMEM_LIMIT_BYTES = 56 * 1024 * 1024


def _cparams(ngrid):
    return pltpu.CompilerParams(dimension_semantics=("arbitrary",) * ngrid,
                                vmem_limit_bytes=VMEM_LIMIT_BYTES)


def _resident(shape, block_index=None):
    idx = (0,) * len(shape) if block_index is None else tuple(block_index)
    return pl.BlockSpec(shape, lambda *_: idx, pipeline_mode=pl.Buffered(1))


def _sigmoid(x):
    return 1.0 / (1.0 + jnp.exp(-x))


def _silu(x):
    return x * _sigmoid(x)


def _gelu(x):
    k = -2.0 * math.sqrt(2.0 / math.pi)
    return x / (1.0 + jnp.exp(x * (k + (k * 0.044715) * (x * x))))


def _norm_mod(x, g, shift, scale):
    ms = jnp.mean(x * x, axis=-1, keepdims=True)
    y = x * lax.rsqrt(ms + RMS_EPS) * g
    return y * (1.0 + scale) + shift


def _layer_norm(x, g, b):
    mu = jnp.mean(x, axis=-1, keepdims=True)
    xc = x - mu
    var = jnp.mean(xc * xc, axis=-1, keepdims=True)
    return xc * lax.rsqrt(var + LN_EPS) * g + b


def _ada_body(s_ref, w_ref, b_ref, o_ref):
    s = _silu(s_ref[...])
    o_ref[0] = jnp.dot(s, w_ref[0], preferred_element_type=F32, precision=HIGHEST) + b_ref[0]


def _ada_call(s_rows, ada_w, ada_b):
    depth, d, n = ada_w.shape
    tn = 1024
    return pl.pallas_call(
        _ada_body,
        grid=(depth, n // tn),
        in_specs=[pl.BlockSpec((SUBLANES, d), lambda l, j: (0, 0)),
                  pl.BlockSpec((1, d, tn), lambda l, j: (l, 0, j)),
                  pl.BlockSpec((1, 1, tn), lambda l, j: (l, 0, j))],
        out_specs=pl.BlockSpec((1, SUBLANES, tn), lambda l, j: (l, 0, j)),
        out_shape=jax.ShapeDtypeStruct((depth, SUBLANES, n), F32),
        compiler_params=_cparams(2),
        name="ada_mod",
    )(s_rows, ada_w, ada_b.reshape(depth, 1, n))


def _even_in_body(x_ref, g_ref, sh_ref, sc_ref, wv_ref, wg_ref, wut_ref, z_ref, ut_ref):
    hn = _norm_mod(x_ref[...], g_ref[...], sh_ref[0], sc_ref[0]).astype(BF16)
    av = jnp.dot(hn, wv_ref[...], preferred_element_type=F32)
    ag = jnp.dot(hn, wg_ref[...], preferred_element_type=F32)
    z_ref[...] = av * _sigmoid(ag)
    ut_ref[...] = lax.dot_general(wut_ref[...], hn, (((1,), (1,)), ((), ())),
                                  preferred_element_type=F32)


def _even_in_call(x2d, g, shift, scale, w_in, wut, cw, seq_len, tm):
    tokens, d = x2d.shape
    sw = wut.shape[0]
    tiles_per_seq = seq_len // tm
    mod_spec = pl.BlockSpec((1, 1, d), lambda i: (i // tiles_per_seq, 0, 0))
    return pl.pallas_call(
        _even_in_body,
        grid=(tokens // tm,),
        in_specs=[pl.BlockSpec((tm, d), lambda i: (i, 0)),
                  _resident((1, d)), mod_spec, mod_spec,
                  _resident((d, cw), (0, 0)), _resident((d, cw), (0, 1)), _resident((sw, d))],
        out_specs=[pl.BlockSpec((tm, cw), lambda i: (i, 0)),
                   pl.BlockSpec((sw, tm), lambda i: (0, i))],
        out_shape=[jax.ShapeDtypeStruct((tokens, cw), F32),
                   jax.ShapeDtypeStruct((sw, tokens), F32)],
        compiler_params=_cparams(1),
        name="even_in_proj",
    )(x2d, g, shift, scale, w_in, w_in, wut)


def _ctx_in_body(x_ref, g_ref, sh_ref, sc_ref, wut_ref, ut_ref):
    hn = _norm_mod(x_ref[...], g_ref[...], sh_ref[...], sc_ref[...]).astype(BF16)
    ut_ref[...] = lax.dot_general(wut_ref[...], hn, (((1,), (1,)), ((), ())),
                                  preferred_element_type=F32)


def _ctx_in_call(c2d, g, shift, scale, wut):
    tokens, d = c2d.shape
    sw = wut.shape[0]
    return pl.pallas_call(
        _ctx_in_body,
        out_shape=jax.ShapeDtypeStruct((sw, tokens), F32),
        compiler_params=pltpu.CompilerParams(vmem_limit_bytes=VMEM_LIMIT_BYTES),
        name="ctx_in_proj",
    )(c2d, g, shift, scale, wut)


CONV_ROWS = 128


def _conv_body(zp_ref, zm_ref, zn_ref, w_ref, b_ref, g_ref, beta_ref, o_ref, buf_ref, acc_ref,
               *, tl, tiles_per_seq):
    i = pl.program_id(0)
    first = (i % tiles_per_seq) == 0
    last = (i % tiles_per_seq) == tiles_per_seq - 1
    buf_ref[0:CONV_HALO, :] = jnp.where(first, 0.0, zp_ref[...])
    buf_ref[CONV_HALO:CONV_HALO + tl, :] = zm_ref[...]
    buf_ref[CONV_HALO + tl:2 * CONV_HALO + tl, :] = jnp.where(last, 0.0, zn_ref[...])

    cw = o_ref.shape[1]
    n_rb = tl // CONV_ROWS
    n_cb = cw // LANES
    rows = CONV_ROWS + SUBLANES

    def lane_block(r0, c0):
        acc = jnp.zeros((CONV_ROWS, LANES), F32)
        for r in range(SUBLANES):
            part = None
            for a in range(4):
                k = SUBLANES * a + r - 1
                if k < 0 or k >= CONV_K:
                    continue
                term = buf_ref[pl.ds(r0 + SUBLANES * a, rows), c0:c0 + LANES] * \
                    w_ref[k:k + 1, c0:c0 + LANES]
                part = term if part is None else part + term
            if r:
                part = pltpu.roll(part, rows - r, axis=0)
            acc = acc + part[:CONV_ROWS]
        acc_ref[pl.ds(r0, CONV_ROWS), c0:c0 + LANES] = acc + b_ref[:, c0:c0 + LANES]

    def block(rb, carry):
        r0 = pl.multiple_of(rb * CONV_ROWS, CONV_ROWS)
        for cb in range(n_cb):
            lane_block(r0, cb * LANES)
        return carry

    lax.fori_loop(0, n_rb, block, 0)

    ln_rows = 128

    def ln_block(rb, carry):
        r0 = pl.multiple_of(rb * ln_rows, ln_rows)
        y = _layer_norm(acc_ref[pl.ds(r0, ln_rows), :], g_ref[...], beta_ref[...])
        o_ref[pl.ds(r0, ln_rows), :] = _silu(y).astype(o_ref.dtype)
        return carry

    lax.fori_loop(0, tl // ln_rows, ln_block, 0)


def _conv_call(z2d, conv_w, conv_b, ln_g, ln_b, seq_len, tl):
    tokens, cw = z2d.shape
    tiles_per_seq = seq_len // tl
    hb = tl // CONV_HALO
    n_hblocks = tokens // CONV_HALO
    w_pad = jnp.pad(conv_w, ((0, 32 - CONV_K), (0, 0)))
    body = functools.partial(_conv_body, tl=tl, tiles_per_seq=tiles_per_seq)
    return pl.pallas_call(
        body,
        grid=(tokens // tl,),
        in_specs=[pl.BlockSpec((CONV_HALO, cw), lambda i: (jnp.maximum(i * hb - 1, 0), 0)),
                  pl.BlockSpec((tl, cw), lambda i: (i, 0)),
                  pl.BlockSpec((CONV_HALO, cw), lambda i: (jnp.minimum((i + 1) * hb, n_hblocks - 1), 0)),
                  _resident((32, cw)), _resident((1, cw)), _resident((1, cw)), _resident((1, cw))],
        out_specs=pl.BlockSpec((tl, cw), lambda i: (i, 0)),
        out_shape=jax.ShapeDtypeStruct((tokens, cw), BF16),
        scratch_shapes=[pltpu.VMEM((tl + 2 * CONV_HALO, cw), F32), pltpu.VMEM((tl, cw), F32)],
        compiler_params=_cparams(1),
        name="conformer_conv",
    )(z2d, z2d, z2d, w_pad, conv_b.reshape(1, cw), ln_g.reshape(1, cw), ln_b.reshape(1, cw))


def _s5_prep_body(ar_row_ref, ai_row_ref, ar_col_ref, ai_col_ref, ls_ref,
                  bt_re_ref, bt_im_ref, c_re_ref, c_im_ref, ct_re_ref, ct_im_ref, dpq_ref,
                  toep_ref, bs_ref, wst_ref, coef_ref, kv_ref):
    t = T_CHUNK
    n = SSM_N
    p_dim = SSM_P
    lane2 = lax.broadcasted_iota(jnp.int32, (1, 2 * t), 1)
    lane1 = lax.broadcasted_iota(jnp.int32, (1, t), 1).astype(F32)
    sub1 = lax.broadcasted_iota(jnp.int32, (t, 1), 0).astype(F32)

    bs_ref[...] = jnp.zeros(bs_ref.shape, bs_ref.dtype)
    wst_ref[...] = jnp.zeros(wst_ref.shape, wst_ref.dtype)
    coef_ref[...] = jnp.zeros(coef_ref.shape, coef_ref.dtype)

    kv = dpq_ref[0] * (lane2 == t).astype(F32)
    for d in range(2):
        dt = jnp.exp(ls_ref[0, d])
        ar = ar_row_ref[0, d]
        ai = ai_row_ref[0, d]
        lam_row = ar * dt
        ang_row = ai * dt
        lam_col = ar_col_ref[0, d] * dt
        ang_col = ai_col_ref[0, d] * dt
        mag = jnp.exp(lam_row)
        ab_re = mag * jnp.cos(ang_row)
        ab_im = mag * jnp.sin(ang_row)
        den = ar * ar + ai * ai
        num_re = ab_re - 1.0
        f_re = (num_re * ar + ab_im * ai) / den
        f_im = (ab_im * ar - num_re * ai) / den
        bbt_re = f_re * bt_re_ref[0, d] - f_im * bt_im_ref[0, d]
        bbt_im = f_re * bt_im_ref[0, d] + f_im * bt_re_ref[0, d]

        if d == 0:
            lag2 = jnp.maximum(lane2 - t, 0).astype(F32)
            msk2 = (lane2 >= t).astype(F32)
        else:
            lag2 = jnp.maximum(t - lane2, 0).astype(F32)
            msk2 = (lane2 <= t).astype(F32)
        pm = jnp.exp(lam_col * lag2) * msk2
        pt_re = pm * jnp.cos(ang_col * lag2)
        pt_im = pm * jnp.sin(ang_col * lag2)
        cb_re = jnp.concatenate(
            [c_re_ref[0, d, pl.ds(p, 1), :] * bbt_re - c_im_ref[0, d, pl.ds(p, 1), :] * bbt_im
             for p in range(p_dim)], axis=0)
        cb_im = jnp.concatenate(
            [c_re_ref[0, d, pl.ds(p, 1), :] * bbt_im + c_im_ref[0, d, pl.ds(p, 1), :] * bbt_re
             for p in range(p_dim)], axis=0)
        kv = kv + (jnp.dot(cb_re, pt_re, preferred_element_type=F32, precision=HIGHEST)
                   - jnp.dot(cb_im, pt_im, preferred_element_type=F32, precision=HIGHEST))

        lag_col = (t - 1.0 - sub1) if d == 0 else sub1
        pw_m = jnp.exp(lag_col * lam_row)
        pw_re = pw_m * jnp.cos(lag_col * ang_row)
        pw_im = pw_m * jnp.sin(lag_col * ang_row)
        for q in range(p_dim):
            br = bbt_re[q:q + 1, :]
            bi = bbt_im[q:q + 1, :]
            bs_ref[0, q * t:(q + 1) * t, (2 * d) * LANES:(2 * d) * LANES + n] = \
                (pw_re * br - pw_im * bi).astype(bs_ref.dtype)
            bs_ref[0, q * t:(q + 1) * t, (2 * d + 1) * LANES:(2 * d + 1) * LANES + n] = \
                (pw_re * bi + pw_im * br).astype(bs_ref.dtype)

        lag_row = (lane1 + 1.0) if d == 0 else (t - lane1)
        qm = jnp.exp(lam_col * lag_row)
        q_re = qm * jnp.cos(ang_col * lag_row)
        q_im = qm * jnp.sin(ang_col * lag_row)
        for p in range(p_dim):
            cr = ct_re_ref[0, d, :, p:p + 1]
            ci = ct_im_ref[0, d, :, p:p + 1]
            wst_ref[0, (2 * d) * LANES:(2 * d) * LANES + n, p * t:(p + 1) * t] = \
                (cr * q_re - ci * q_im).astype(wst_ref.dtype)
            wst_ref[0, (2 * d + 1) * LANES:(2 * d + 1) * LANES + n, p * t:(p + 1) * t] = \
                (-(cr * q_im + ci * q_re)).astype(wst_ref.dtype)

        tt = float(t)
        at_m = jnp.exp(lam_row * tt)
        coef_ref[0, 2 * d:2 * d + 1, 0:n] = at_m * jnp.cos(ang_row * tt)
        coef_ref[0, 2 * d + 1:2 * d + 2, 0:n] = at_m * jnp.sin(ang_row * tt)

    kv_ref[...] = kv

    def toep_block(p, carry):
        kv_p = kv_ref[pl.ds(pl.multiple_of(p * p_dim, p_dim), p_dim), :]
        for q in range(p_dim):
            row = jnp.broadcast_to(kv_p[q:q + 1, :], (t, 2 * t))
            rolled = pltpu.roll(row, 0, axis=1, stride=1, stride_axis=0)
            toep_ref[0, q * t:(q + 1) * t, pl.ds(pl.multiple_of(p * t, t), t)] = \
                rolled[:, t:].astype(toep_ref.dtype)
        return carry

    lax.fori_loop(0, p_dim, toep_block, 0)


def _s5_prep_call(a_re, a_im, log_step, b_re, b_im, c_re, c_im, d_skip):
    g_dim = a_re.shape[1]
    n, p_dim, t = SSM_N, SSM_P, T_CHUNK
    gm = lambda v: jnp.swapaxes(v, 0, 1)
    ar, ai = gm(a_re), gm(a_im)
    dpq = (jnp.eye(p_dim, dtype=F32)[None] * d_skip.reshape(g_dim, p_dim, 1)).reshape(g_dim, p_dim * p_dim, 1)
    args = (ar[:, :, None, :], ai[:, :, None, :], ar[:, :, :, None], ai[:, :, :, None],
            gm(log_step)[:, :, None, None],
            jnp.swapaxes(gm(b_re), 2, 3), jnp.swapaxes(gm(b_im), 2, 3),
            gm(c_re), gm(c_im), jnp.swapaxes(gm(c_re), 2, 3), jnp.swapaxes(gm(c_im), 2, 3), dpq)

    def spec(v):
        blk = (1,) + v.shape[1:]
        nd = v.ndim
        return pl.BlockSpec(blk, lambda g: (g,) + (0,) * (nd - 1))

    out_shapes = [jax.ShapeDtypeStruct((g_dim, p_dim * t, p_dim * t), BF16),
                  jax.ShapeDtypeStruct((g_dim, p_dim * t, 4 * LANES), BF16),
                  jax.ShapeDtypeStruct((g_dim, 4 * LANES, p_dim * t), BF16),
                  jax.ShapeDtypeStruct((g_dim, SUBLANES, LANES), F32)]
    return pl.pallas_call(
        _s5_prep_body,
        grid=(g_dim,),
        in_specs=[spec(v) for v in args],
        out_specs=[spec(s) for s in out_shapes],
        out_shape=out_shapes,
        scratch_shapes=[pltpu.VMEM((p_dim * p_dim, 2 * t), F32)],
        compiler_params=_cparams(1),
        name="s5_prep",
    )(*args)


def _s5_body(*refs, nb, nc, with_y):
    if with_y:
        u_ref, bs_ref, coef_ref, h0_ref, toep_ref, wst_ref, y_ref, fin_ref, s_scr, h_scr = refs
    else:
        u_ref, bs_ref, coef_ref, h0_ref, fin_ref, s_scr = refs
    p_dim = u_ref.shape[1]
    t = T_CHUNK
    u = jnp.concatenate([u_ref[0, q] for q in range(p_dim)], axis=1).astype(BF16)
    s_scr[...] = jnp.dot(u, bs_ref[0], preferred_element_type=F32)
    if with_y:
        h_scr[...] = jnp.zeros(h_scr.shape, h_scr.dtype)

    af_re, af_im = coef_ref[0, 0:1, :], coef_ref[0, 1:2, :]
    ab_re, ab_im = coef_ref[0, 2:3, :], coef_ref[0, 3:4, :]
    init = tuple(h0_ref[0, r:r + 1, :] for r in range(4 * nb))

    blk = SUBLANES if nc % SUBLANES == 0 else nc

    def block(k, carry):
        new = list(carry)
        for b in range(nb):
            f0 = b * nc + blk * k
            b0 = b * nc + nc - blk - blk * k
            if blk == SUBLANES:
                f0 = pl.multiple_of(f0, SUBLANES)
                b0 = pl.multiple_of(b0, SUBLANES)
            sf_re = s_scr[pl.ds(f0, blk), 0:LANES]
            sf_im = s_scr[pl.ds(f0, blk), LANES:2 * LANES]
            sb_re = s_scr[pl.ds(b0, blk), 2 * LANES:3 * LANES]
            sb_im = s_scr[pl.ds(b0, blk), 3 * LANES:4 * LANES]
            fr, fi, br, bi = new[4 * b:4 * b + 4]
            hf_re, hf_im, hb_re, hb_im = [], [], [], []
            for j in range(blk):
                jb = blk - 1 - j
                hf_re.append(fr)
                hf_im.append(fi)
                hb_re.insert(0, br)
                hb_im.insert(0, bi)
                fr, fi = (af_re * fr - af_im * fi + sf_re[j:j + 1],
                          af_re * fi + af_im * fr + sf_im[j:j + 1])
                br, bi = (ab_re * br - ab_im * bi + sb_re[jb:jb + 1],
                          ab_re * bi + ab_im * br + sb_im[jb:jb + 1])
            if with_y:
                h_scr[pl.ds(f0, blk), 0:LANES] = jnp.concatenate(hf_re, axis=0)
                h_scr[pl.ds(f0, blk), LANES:2 * LANES] = jnp.concatenate(hf_im, axis=0)
                h_scr[pl.ds(b0, blk), 2 * LANES:3 * LANES] = jnp.concatenate(hb_re, axis=0)
                h_scr[pl.ds(b0, blk), 3 * LANES:4 * LANES] = jnp.concatenate(hb_im, axis=0)
            new[4 * b:4 * b + 4] = [fr, fi, br, bi]
        return tuple(new)

    if nc == blk:
        fin = block(0, init)
    else:
        fin = lax.fori_loop(0, nc // blk, block, init)
    fin_ref[0] = jnp.concatenate(list(fin) + [jnp.zeros((1, LANES), F32)] * (SUBLANES - 4 * nb), axis=0) \
        if 4 * nb < SUBLANES else jnp.concatenate(list(fin), axis=0)

    if with_y:
        y = jnp.dot(u, toep_ref[0], preferred_element_type=F32)
        y = y + jnp.dot(h_scr[...].astype(BF16), wst_ref[0], preferred_element_type=F32)
        y = _gelu(y)
        for p in range(p_dim):
            y_ref[0, p] = y[:, p * t:(p + 1) * t]


def _s5_call(u4, bs, coef, h0, toep=None, wst=None, *, nb, nc):
    g_dim, p_dim, rows, t = u4.shape
    with_y = toep is not None
    body = functools.partial(_s5_body, nb=nb, nc=nc, with_y=with_y)

    def spec(v):
        blk = (1,) + v.shape[1:]
        nd = v.ndim
        return pl.BlockSpec(blk, lambda g: (g,) + (0,) * (nd - 1))

    fin_shape = jax.ShapeDtypeStruct((g_dim, SUBLANES, LANES), F32)
    args = [u4, bs, coef, h0]
    scratch = [pltpu.VMEM((rows, 4 * LANES), F32)]
    if with_y:
        args += [toep, wst]
        out_shape = [jax.ShapeDtypeStruct(u4.shape, F32), fin_shape]
        scratch.append(pltpu.VMEM((rows, 4 * LANES), F32))
    else:
        out_shape = [fin_shape]
    outs = pl.pallas_call(
        body,
        grid=(g_dim,),
        in_specs=[spec(v) for v in args],
        out_specs=[spec(s) for s in out_shape],
        out_shape=out_shape,
        scratch_shapes=scratch,
        compiler_params=_cparams(1),
        name="s5_chunks" if with_y else "s5_ctx_states",
    )(*args)
    return outs


def _even_out_body(yc_ref, yt_ref, gwt_ref, gb_ref, woa_ref, wob_ref, x_ref, g1_ref, o_ref):
    yt = yt_ref[...]
    gl = jnp.dot(gwt_ref[...], yt.astype(BF16), preferred_element_type=F32) + gb_ref[...]
    ys = (yt * _sigmoid(gl)).astype(BF16)
    mix = jnp.dot(yc_ref[...], woa_ref[...], preferred_element_type=F32)
    mix = mix + jnp.dot(ys.T, wob_ref[...], preferred_element_type=F32)
    o_ref[...] = x_ref[...] + g1_ref[0] * mix


def _even_out_call(yc, yt, gwt, gb, w_out, x2d, g1, seq_len, tm):
    tokens, d = x2d.shape
    cw = yc.shape[1]
    sw = yt.shape[0]
    tiles_per_seq = seq_len // tm
    return pl.pallas_call(
        _even_out_body,
        grid=(tokens // tm,),
        in_specs=[pl.BlockSpec((tm, cw), lambda i: (i, 0)),
                  pl.BlockSpec((sw, tm), lambda i: (0, i)),
                  _resident((sw, sw)), _resident((sw, 1)),
                  _resident((cw, d), (0, 0)), _resident((sw, d), (cw // sw, 0)),
                  pl.BlockSpec((tm, d), lambda i: (i, 0)),
                  pl.BlockSpec((1, 1, d), lambda i: (i // tiles_per_seq, 0, 0))],
        out_specs=pl.BlockSpec((tm, d), lambda i: (i, 0)),
        out_shape=jax.ShapeDtypeStruct((tokens, d), F32),
        compiler_params=_cparams(1),
        name="even_out_proj",
    )(yc, yt, gwt, gb, w_out, w_out, x2d, g1)


def _odd_body(x_ref, g_ref, sh_ref, sc_ref, wu_ref, wv_ref, lg_ref, lb_ref, sw_ref, sb_ref,
              wo_ref, g1_ref, o_ref, gated_ref):
    x = x_ref[...]
    tm = x.shape[0]
    hn = _norm_mod(x, g_ref[...], sh_ref[0], sc_ref[0]).astype(BF16)
    v = _gelu(jnp.dot(hn, wv_ref[...], preferred_element_type=F32))
    v = _layer_norm(v, lg_ref[...], lb_ref[...]).astype(BF16)
    u = _gelu(jnp.dot(hn, wu_ref[...], preferred_element_type=F32))
    hd = v.shape[1] // SGU_HEADS
    for ch in range(tm // SGU_CHUNK):
        r0 = ch * SGU_CHUNK
        for h in range(SGU_HEADS):
            s = jnp.dot(sw_ref[h], v[r0:r0 + SGU_CHUNK, h * hd:(h + 1) * hd],
                        preferred_element_type=F32) + sb_ref[h]
            gated_ref[r0:r0 + SGU_CHUNK, h * hd:(h + 1) * hd] = \
                (u[r0:r0 + SGU_CHUNK, h * hd:(h + 1) * hd] * s).astype(BF16)
    mix = jnp.dot(gated_ref[...], wo_ref[...], preferred_element_type=F32)
    o_ref[...] = x + g1_ref[0] * mix


def _odd_call(x2d, g, shift, scale, w_in, ln_g, ln_b, sgu_w, sgu_b, wo, g1, seq_len, tm):
    tokens, d = x2d.shape
    w = w_in.shape[1] // 2
    tiles_per_seq = seq_len // tm
    mod_spec = pl.BlockSpec((1, 1, d), lambda i: (i // tiles_per_seq, 0, 0))
    return pl.pallas_call(
        _odd_body,
        grid=(tokens // tm,),
        in_specs=[pl.BlockSpec((tm, d), lambda i: (i, 0)),
                  _resident((1, d)), mod_spec, mod_spec,
                  _resident((d, w), (0, 0)), _resident((d, w), (0, 1)), _resident((1, w)), _resident((1, w)),
                  _resident(sgu_w.shape), _resident(sgu_b.shape),
                  _resident((w, d)), mod_spec],
        out_specs=pl.BlockSpec((tm, d), lambda i: (i, 0)),
        out_shape=jax.ShapeDtypeStruct((tokens, d), F32),
        scratch_shapes=[pltpu.VMEM((tm, w), BF16)],
        compiler_params=_cparams(1),
        name="odd_gmlp",
    )(x2d, g, shift, scale, w_in, w_in, ln_g, ln_b, sgu_w, sgu_b, wo, g1)


def _row_from_left(v):
    n_vr = GRID_W // SUBLANES
    first = lax.broadcasted_iota(jnp.int32, (SUBLANES, LANES), 0) == 0
    r = [pltpu.roll(v[SUBLANES * k:SUBLANES * (k + 1)], 1, axis=0) for k in range(n_vr)]
    return jnp.concatenate([jnp.where(first, r[k - 1] if k else 0.0, r[k]) for k in range(n_vr)], axis=0)


def _row_from_right(v):
    n_vr = GRID_W // SUBLANES
    last = lax.broadcasted_iota(jnp.int32, (SUBLANES, LANES), 0) == SUBLANES - 1
    r = [pltpu.roll(v[SUBLANES * k:SUBLANES * (k + 1)], SUBLANES - 1, axis=0) for k in range(n_vr)]
    return jnp.concatenate([jnp.where(last, r[k + 1] if k < n_vr - 1 else 0.0, r[k])
                            for k in range(n_vr)], axis=0)


def _conv3x3_row(up, mid, dn, cw_ref, cb_ref, c0):
    w = lambda k: cw_ref[pl.ds(k, 1), pl.ds(c0, LANES)]
    left = up * w(0) + mid * w(3) + dn * w(6)
    cent = up * w(1) + mid * w(4) + dn * w(7)
    right = up * w(2) + mid * w(5) + dn * w(8)
    return cent + _row_from_left(left) + _row_from_right(right) + cb_ref[:, pl.ds(c0, LANES)]


def _ffn_up_body(x_ref, g_ref, sh_ref, sc_ref, wa_ref, wg_ref, cw_ref, cb_ref, act_ref, zb_ref,
                 hn_ref, z_ref):
    @pl.when(pl.program_id(1) == 0)
    def _():
        hn_ref[...] = _norm_mod(x_ref[...], g_ref[...], sh_ref[0], sc_ref[0]).astype(BF16)

    tm, tn = z_ref.shape
    half = tn // 2
    n_rows = tm // GRID_W
    n_cb = half // LANES
    z_ref[:, 0:half] = jnp.dot(hn_ref[...], wa_ref[...], preferred_element_type=F32)
    z_ref[:, half:tn] = jnp.dot(hn_ref[...], wg_ref[...], preferred_element_type=F32)

    for e, r in enumerate((0, 1, n_rows - 2, n_rows - 1)):
        zb_ref[0, e] = z_ref[r * GRID_W:(r + 1) * GRID_W, :].astype(zb_ref.dtype)
    act_ref[0:GRID_W, :] = jnp.zeros((GRID_W, half), act_ref.dtype)
    act_ref[tm - GRID_W:tm, :] = jnp.zeros((GRID_W, half), act_ref.dtype)

    def item(r, carry):
        r0 = pl.multiple_of((1 + r) * GRID_W, GRID_W)

        def conv(c0):
            rows = [z_ref[pl.ds(r0 + dy * GRID_W, GRID_W), c0:c0 + LANES] for dy in (-1, 0, 1)]
            return _conv3x3_row(*rows, cw_ref, cb_ref, c0)

        for cb in range(n_cb):
            a = conv(cb * LANES)
            g = conv(half + cb * LANES)
            act_ref[pl.ds(r0, GRID_W), cb * LANES:(cb + 1) * LANES] = (a * _silu(g)).astype(act_ref.dtype)
        return carry

    lax.fori_loop(0, n_rows - 2, item, 0)


def _ffn_up_call(x2d, g, shift, scale, w, cw, cb, seq_len, tm, tn):
    tokens, d = x2d.shape
    n2 = w.shape[1]
    half = tn // 2
    gate_blk0 = (n2 // 2) // half
    tiles_per_seq = seq_len // tm
    mod_spec = pl.BlockSpec((1, 1, d), lambda i, j: (i // tiles_per_seq, 0, 0))
    return pl.pallas_call(
        _ffn_up_body,
        grid=(tokens // tm, n2 // tn),
        in_specs=[pl.BlockSpec((tm, d), lambda i, j: (i, 0)),
                  _resident((1, d)), mod_spec, mod_spec,
                  pl.BlockSpec((d, half), lambda i, j: (0, j)),
                  pl.BlockSpec((d, half), lambda i, j: (0, gate_blk0 + j)),
                  pl.BlockSpec((16, tn), lambda i, j: (0, j)),
                  pl.BlockSpec((1, tn), lambda i, j: (0, j))],
        out_specs=[pl.BlockSpec((tm, half), lambda i, j: (i, j)),
                   pl.BlockSpec((1, 4, GRID_W, tn), lambda i, j: (i, 0, 0, j))],
        out_shape=[jax.ShapeDtypeStruct((tokens, n2 // 2), BF16),
                   jax.ShapeDtypeStruct((tokens // tm, 4, GRID_W, n2), BF16)],
        scratch_shapes=[pltpu.VMEM((tm, d), BF16), pltpu.VMEM((tm, tn), F32)],
        compiler_params=_cparams(2),
        name="ffn_up_conv",
    )(x2d, g, shift, scale, w, w, cw, cb)


def _ffn_down_body(act_ref, zc_ref, zp_ref, zn_ref, cw_ref, cb_ref, wo_ref, x_ref, g2_ref, fg_ref, o_ref,
                   acts_ref, *, td, tn, down_per_up, up_per_seq, final_norm):
    k = pl.program_id(0)
    up_tile = k // down_per_up
    pos = k % down_per_up
    top_ok = jnp.where((up_tile % up_per_seq) == 0, 0.0, 1.0)
    bot_ok = jnp.where((up_tile % up_per_seq) == up_per_seq - 1, 0.0, 1.0)
    dff = acts_ref.shape[1]
    half = tn // 2
    acts_ref[...] = act_ref[...]

    def edge_row(rows, dst):
        def blk(j, carry):
            for s in range(half // LANES):
                cz = pl.multiple_of(j * tn + s * LANES, LANES)
                cg = pl.multiple_of(cz + half, LANES)
                a = _conv3x3_row(*rows(cz), cw_ref, cb_ref, cz)
                g = _conv3x3_row(*rows(cg), cw_ref, cb_ref, cg)
                acts_ref[dst:dst + GRID_W, pl.ds(pl.multiple_of(j * half + s * LANES, LANES), LANES)] = \
                    (a * _silu(g)).astype(acts_ref.dtype)
            return carry

        lax.fori_loop(0, 2 * dff // tn, blk, 0)

    @pl.when(pos == 0)
    def _():
        edge_row(lambda c: (zp_ref[0, 0, :, pl.ds(c, LANES)].astype(F32) * top_ok,
                            zc_ref[0, 0, :, pl.ds(c, LANES)].astype(F32),
                            zc_ref[0, 1, :, pl.ds(c, LANES)].astype(F32)), 0)

    @pl.when(pos == down_per_up - 1)
    def _():
        edge_row(lambda c: (zc_ref[0, 2, :, pl.ds(c, LANES)].astype(F32),
                            zc_ref[0, 3, :, pl.ds(c, LANES)].astype(F32),
                            zn_ref[0, 0, :, pl.ds(c, LANES)].astype(F32) * bot_ok), td - GRID_W)

    y = x_ref[...] + g2_ref[0] * jnp.dot(acts_ref[...], wo_ref[...], preferred_element_type=F32)
    if final_norm:
        ms = jnp.mean(y * y, axis=-1, keepdims=True)
        y = y * lax.rsqrt(ms + RMS_EPS) * fg_ref[...]
    o_ref[...] = y


def _ffn_down_call(act, zb, cw, cb, wo, x2d, g2, final_g, seq_len, tm, tn, td, final_norm):
    tokens, d = x2d.shape
    dff = act.shape[1]
    n2 = zb.shape[3]
    n_up = zb.shape[0]
    down_per_up = tm // td
    tiles_per_seq = seq_len // td
    body = functools.partial(_ffn_down_body, td=td, tn=tn, down_per_up=down_per_up,
                             up_per_seq=seq_len // tm, final_norm=final_norm)
    return pl.pallas_call(
        body,
        grid=(tokens // td,),
        in_specs=[pl.BlockSpec((td, dff), lambda k: (k, 0)),
                  pl.BlockSpec((1, 4, GRID_W, n2), lambda k: (k // down_per_up, 0, 0, 0)),
                  pl.BlockSpec((1, 1, GRID_W, n2),
                               lambda k: (jnp.maximum(k // down_per_up - 1, 0), 3, 0, 0)),
                  pl.BlockSpec((1, 1, GRID_W, n2),
                               lambda k: (jnp.minimum(k // down_per_up + 1, n_up - 1), 0, 0, 0)),
                  _resident((16, n2)), _resident((1, n2)), _resident((dff, d)),
                  pl.BlockSpec((td, d), lambda k: (k, 0)),
                  pl.BlockSpec((1, 1, d), lambda k: (k // tiles_per_seq, 0, 0)),
                  _resident((1, d))],
        out_specs=pl.BlockSpec((td, d), lambda k: (k, 0)),
        out_shape=jax.ShapeDtypeStruct((tokens, d), F32),
        scratch_shapes=[pltpu.VMEM((td, dff), BF16)],
        compiler_params=_cparams(1),
        name="ffn_edge_down",
    )(act, zb, zb, zb, cw, cb, wo, x2d, g2, final_g.reshape(1, d))


def _pick_tile(seq_len, want):
    tm = min(want, seq_len)
    assert seq_len % tm == 0
    return tm


def kernel(x, c, ctx, c_ctx, ada_w, ada_b, norm_g, final_g, e_w_in, e_conv_w, e_conv_b, e_ln_g, e_ln_b, s5_a_re, s5_a_im, s5_log_step, s5_b_re, s5_b_im, s5_c_re, s5_c_im, s5_d, s5_glu_w, s5_glu_b, e_w_out, o_w_in, o_ln_g, o_ln_b, o_sgu_w, o_sgu_b, o_w_out, f_w_in, f_conv_w, f_conv_b, f_w_out):
    bn, seq_len, d = x.shape
    ctx_len = ctx.shape[1]
    depth = ada_w.shape[0]
    assert depth == 2 and bn == 2, "block structure is specialised to depth 2, batch 2"
    tokens = bn * seq_len
    cw = e_conv_w.shape[2]
    sw = s5_d.shape[1]
    g_dim = sw // SSM_P
    t = T_CHUNK
    nc = seq_len // t
    nc_ctx = ctx_len // t
    assert seq_len % t == 0 and ctx_len % t == 0 and seq_len % GRID_W == 0
    assert cw % sw == 0, "out-proj row blocks are addressed in units of the S5 width"

    x2d = x.reshape(tokens, d)

    s_rows = jnp.zeros((SUBLANES, d), F32).at[:bn].set(c).at[bn].set(c_ctx)
    mods = _ada_call(s_rows, ada_w, ada_b)

    def lat_mod(layer, k):
        return mods[layer, :bn, k * d:(k + 1) * d].reshape(bn, 1, d)

    li = 0
    w_in = e_w_in[li]
    w_in_bf = w_in.astype(BF16)
    wut = w_in[:, 2 * cw:].T.astype(BF16)
    toep, bs, wst, coef = _s5_prep_call(s5_a_re[li], s5_a_im[li], s5_log_step[li], s5_b_re[li],
                                        s5_b_im[li], s5_c_re[li], s5_c_im[li], s5_d[li])

    g_row = norm_g[0, 0].reshape(1, d)
    ctx_shift = mods[0, bn:bn + 1, 0:d]
    ctx_scale = mods[0, bn:bn + 1, d:2 * d]
    uct = _ctx_in_call(ctx.reshape(bn * ctx_len, d), g_row, ctx_shift, ctx_scale, wut)
    uc4 = uct.reshape(g_dim, SSM_P, bn * nc_ctx, t)
    rows_ctx = -(-(bn * nc_ctx) // SUBLANES) * SUBLANES
    uc4 = jnp.pad(uc4, ((0, 0), (0, 0), (0, rows_ctx - bn * nc_ctx), (0, 0)))
    zero_h0 = jnp.zeros((g_dim, SUBLANES, LANES), F32)
    (h0s,) = _s5_call(uc4, bs, coef, zero_h0, nb=bn, nc=nc_ctx)

    tm_in = _pick_tile(seq_len, 512)
    z, ut = _even_in_call(x2d, g_row, lat_mod(0, 0), lat_mod(0, 1), w_in_bf, wut, cw, seq_len, tm_in)
    yc = _conv_call(z, e_conv_w[li], e_conv_b[li], e_ln_g[li], e_ln_b[li], seq_len,
                    _pick_tile(seq_len, 256))
    u4 = ut.reshape(g_dim, SSM_P, bn * nc, t)
    y4, _ = _s5_call(u4, bs, coef, h0s, toep, wst, nb=bn, nc=nc)
    yt = y4.reshape(sw, tokens)
    w_out = e_w_out[li]
    h = _even_out_call(yc, yt, s5_glu_w[li].T.astype(BF16), s5_glu_b[li].reshape(sw, 1),
                       w_out.astype(BF16), x2d, lat_mod(0, 2),
                       seq_len, _pick_tile(seq_len, 512))

    def conv_ffn(h, layer, final_norm):
        tm = _pick_tile(seq_len, 1024)
        tn = 1024
        td = _pick_tile(tm, 256)
        n2 = f_w_in.shape[2]

        def regroup(v):
            lead = v.shape[:-1]
            v = v.reshape(lead + (2, n2 // tn, tn // 2))
            return jnp.swapaxes(v, -3, -2).reshape(lead + (n2,))

        cw = jnp.pad(regroup(f_conv_w[layer].reshape(9, n2)), ((0, 16 - 9), (0, 0)))
        cb = regroup(f_conv_b[layer]).reshape(1, n2)
        act, zb = _ffn_up_call(h, norm_g[layer, 1].reshape(1, d), lat_mod(layer, 3), lat_mod(layer, 4),
                               f_w_in[layer].astype(BF16), cw, cb, seq_len, tm, tn)
        return _ffn_down_call(act, zb, cw, cb, f_w_out[layer].astype(BF16), h, lat_mod(layer, 5),
                              final_g, seq_len, tm, tn, td, final_norm)

    h = conv_ffn(h, 0, False)

    li = 0
    w = o_w_in.shape[2] // 2
    h = _odd_call(h, norm_g[1, 0].reshape(1, d), lat_mod(1, 0), lat_mod(1, 1),
                  o_w_in[li].astype(BF16),
                  o_ln_g[li].reshape(1, w), o_ln_b[li].reshape(1, w),
                  o_sgu_w[li].astype(BF16), o_sgu_b[li].reshape(SGU_HEADS, SGU_CHUNK, 1),
                  o_w_out[li].astype(BF16), lat_mod(1, 2), seq_len, _pick_tile(seq_len, 256))
    h = conv_ffn(h, 1, True)
    return h.reshape(bn, seq_len, d)
```

```python
import functools
import math

import jax
import jax.numpy as jnp
from jax import lax
from jax.experimental import pallas as pl
from jax.experimental.pallas import tpu as pltpu

F32 = jnp.float32
BF16 = jnp.bfloat16
HIGHEST = lax.Precision.HIGHEST

RMS_EPS = 1e-6
LN_EPS = 1e-5
GRID_W = 64
CONV_K = 31
CONV_HALO = 16
CONV_TAP_ROWS = 32
FFN_TAP_ROWS = 16
SSM_P = 16
SSM_N = 64
SGU_CHUNK = 128
SGU_HEADS = 8
T_CHUNK = 128
LANES = 128
SUBLANES = 8
VMEM_LIMIT_BYTES = 56 * 1024 * 1024


def _cparams(ngrid):
    return pltpu.CompilerParams(dimension_semantics=("arbitrary",) * ngrid,
                                vmem_limit_bytes=VMEM_LIMIT_BYTES)


def _resident(shape, block_index=None):
    idx = (0,) * len(shape) if block_index is None else tuple(block_index)
    return pl.BlockSpec(shape, lambda *_: idx, pipeline_mode=pl.Buffered(1))


def _sigmoid(x):
    return 1.0 / (1.0 + jnp.exp(-x))


def _silu(x):
    return x * _sigmoid(x)


def _gelu(x):
    return 0.5 * x * (1.0 + jnp.tanh(math.sqrt(2.0 / math.pi) * (x + 0.044715 * (x * x * x))))


def _norm_mod(x, g, shift, scale):
    ms = jnp.mean(x * x, axis=-1, keepdims=True)
    y = x * lax.rsqrt(ms + RMS_EPS) * g
    return y * (1.0 + scale) + shift


def _layer_norm(x, g, b):
    mu = jnp.mean(x, axis=-1, keepdims=True)
    xc = x - mu
    var = jnp.mean(xc * xc, axis=-1, keepdims=True)
    return xc * lax.rsqrt(var + LN_EPS) * g + b


def _ada_body(s_ref, w_ref, b_ref, o_ref):
    s = _silu(s_ref[...])
    o_ref[0] = jnp.dot(s, w_ref[0], preferred_element_type=F32, precision=HIGHEST) + b_ref[0]


def _ada_call(s_rows, ada_w, ada_b):
    depth, d, n = ada_w.shape
    tn = 1024
    return pl.pallas_call(
        _ada_body,
        grid=(depth, n // tn),
        in_specs=[pl.BlockSpec((SUBLANES, d), lambda l, j: (0, 0)),
                  pl.BlockSpec((1, d, tn), lambda l, j: (l, 0, j)),
                  pl.BlockSpec((1, 1, tn), lambda l, j: (l, 0, j))],
        out_specs=pl.BlockSpec((1, SUBLANES, tn), lambda l, j: (l, 0, j)),
        out_shape=jax.ShapeDtypeStruct((depth, SUBLANES, n), F32),
        compiler_params=_cparams(2),
        name="ada_mod",
    )(s_rows, ada_w, ada_b.reshape(depth, 1, n))


def _even_in_body(x_ref, g_ref, sh_ref, sc_ref, wv_ref, wg_ref, wut_ref, z_ref, ut_ref):
    hn = _norm_mod(x_ref[...], g_ref[...], sh_ref[0], sc_ref[0]).astype(BF16)
    av = jnp.dot(hn, wv_ref[...], preferred_element_type=F32)
    ag = jnp.dot(hn, wg_ref[...], preferred_element_type=F32)
    z_ref[...] = av * _sigmoid(ag)
    ut_ref[...] = lax.dot_general(wut_ref[...], hn, (((1,), (1,)), ((), ())),
                                  preferred_element_type=F32)


def _even_in_call(x2d, g, shift, scale, w_in, wut, cw, seq_len, tm):
    tokens, d = x2d.shape
    sw = wut.shape[0]
    tiles_per_seq = seq_len // tm
    mod_spec = pl.BlockSpec((1, 1, d), lambda i: (i // tiles_per_seq, 0, 0))
    return pl.pallas_call(
        _even_in_body,
        grid=(tokens // tm,),
        in_specs=[pl.BlockSpec((tm, d), lambda i: (i, 0)),
                  _resident((1, d)), mod_spec, mod_spec,
                  _resident((d, cw), (0, 0)), _resident((d, cw), (0, 1)), _resident((sw, d))],
        out_specs=[pl.BlockSpec((tm, cw), lambda i: (i, 0)),
                   pl.BlockSpec((sw, tm), lambda i: (0, i))],
        out_shape=[jax.ShapeDtypeStruct((tokens, cw), F32),
                   jax.ShapeDtypeStruct((sw, tokens), F32)],
        compiler_params=_cparams(1),
        name="even_in_proj",
    )(x2d, g, shift, scale, w_in, w_in, wut)


def _ctx_in_body(x_ref, g_ref, sh_ref, sc_ref, wut_ref, ut_ref):
    hn = _norm_mod(x_ref[...], g_ref[...], sh_ref[...], sc_ref[...]).astype(BF16)
    ut_ref[...] = lax.dot_general(wut_ref[...], hn, (((1,), (1,)), ((), ())),
                                  preferred_element_type=F32)


def _ctx_in_call(c2d, g, shift, scale, wut):
    tokens, d = c2d.shape
    sw = wut.shape[0]
    return pl.pallas_call(
        _ctx_in_body,
        out_shape=jax.ShapeDtypeStruct((sw, tokens), F32),
        compiler_params=pltpu.CompilerParams(vmem_limit_bytes=VMEM_LIMIT_BYTES),
        name="ctx_in_proj",
    )(c2d, g, shift, scale, wut)


CONV_ROWS = 128


def _conv_body(zp_ref, zm_ref, zn_ref, w_ref, b_ref, g_ref, beta_ref, o_ref, buf_ref, acc_ref,
               *, tl, tiles_per_seq):
    i = pl.program_id(0)
    first = (i % tiles_per_seq) == 0
    last = (i % tiles_per_seq) == tiles_per_seq - 1
    buf_ref[0:CONV_HALO, :] = jnp.where(first, 0.0, zp_ref[...])
    buf_ref[CONV_HALO:CONV_HALO + tl, :] = zm_ref[...]
    buf_ref[CONV_HALO + tl:2 * CONV_HALO + tl, :] = jnp.where(last, 0.0, zn_ref[...])

    cw = o_ref.shape[1]
    n_rb = tl // CONV_ROWS
    n_cb = cw // LANES
    rows = CONV_ROWS + SUBLANES

    def lane_block(r0, c0):
        acc = jnp.zeros((CONV_ROWS, LANES), F32)
        for r in range(SUBLANES):
            part = None
            for a in range(4):
                k = SUBLANES * a + r - 1
                if k < 0 or k >= CONV_K:
                    continue
                term = buf_ref[pl.ds(r0 + SUBLANES * a, rows), c0:c0 + LANES] * \
                    w_ref[k:k + 1, c0:c0 + LANES]
                part = term if part is None else part + term
            if r:
                part = pltpu.roll(part, rows - r, axis=0)
            acc = acc + part[:CONV_ROWS]
        acc_ref[pl.ds(r0, CONV_ROWS), c0:c0 + LANES] = acc + b_ref[:, c0:c0 + LANES]

    def block(rb, carry):
        r0 = pl.multiple_of(rb * CONV_ROWS, CONV_ROWS)
        for cb in range(n_cb):
            lane_block(r0, cb * LANES)
        return carry

    lax.fori_loop(0, n_rb, block, 0)

    ln_rows = 128

    def ln_block(rb, carry):
        r0 = pl.multiple_of(rb * ln_rows, ln_rows)
        y = _layer_norm(acc_ref[pl.ds(r0, ln_rows), :], g_ref[...], beta_ref[...])
        o_ref[pl.ds(r0, ln_rows), :] = _silu(y).astype(o_ref.dtype)
        return carry

    lax.fori_loop(0, tl // ln_rows, ln_block, 0)


def _conv_call(z2d, conv_w, conv_b, ln_g, ln_b, seq_len, tl):
    tokens, cw = z2d.shape
    tiles_per_seq = seq_len // tl
    hb = tl // CONV_HALO
    n_hblocks = tokens // CONV_HALO
    w_pad = jnp.pad(conv_w, ((0, CONV_TAP_ROWS - CONV_K), (0, 0)))
    body = functools.partial(_conv_body, tl=tl, tiles_per_seq=tiles_per_seq)
    return pl.pallas_call(
        body,
        grid=(tokens // tl,),
        in_specs=[pl.BlockSpec((CONV_HALO, cw), lambda i: (jnp.maximum(i * hb - 1, 0), 0)),
                  pl.BlockSpec((tl, cw), lambda i: (i, 0)),
                  pl.BlockSpec((CONV_HALO, cw), lambda i: (jnp.minimum((i + 1) * hb, n_hblocks - 1), 0)),
                  _resident((CONV_TAP_ROWS, cw)), _resident((1, cw)), _resident((1, cw)), _resident((1, cw))],
        out_specs=pl.BlockSpec((tl, cw), lambda i: (i, 0)),
        out_shape=jax.ShapeDtypeStruct((tokens, cw), BF16),
        scratch_shapes=[pltpu.VMEM((tl + 2 * CONV_HALO, cw), F32), pltpu.VMEM((tl, cw), F32)],
        compiler_params=_cparams(1),
        name="conformer_conv",
    )(z2d, z2d, z2d, w_pad, conv_b.reshape(1, cw), ln_g.reshape(1, cw), ln_b.reshape(1, cw))


def _s5_prep_body(ar_row_ref, ai_row_ref, ar_col_ref, ai_col_ref, ls_ref,
                  bt_re_ref, bt_im_ref, c_re_ref, c_im_ref, ct_re_ref, ct_im_ref, dpq_ref,
                  toep_ref, bs_ref, wst_ref, coef_ref, kv_ref):
    t = T_CHUNK
    n = SSM_N
    p_dim = SSM_P
    lane2 = lax.broadcasted_iota(jnp.int32, (1, 2 * t), 1)
    lane1 = lax.broadcasted_iota(jnp.int32, (1, t), 1).astype(F32)
    sub1 = lax.broadcasted_iota(jnp.int32, (t, 1), 0).astype(F32)

    bs_ref[...] = jnp.zeros(bs_ref.shape, bs_ref.dtype)
    wst_ref[...] = jnp.zeros(wst_ref.shape, wst_ref.dtype)
    coef_ref[...] = jnp.zeros(coef_ref.shape, coef_ref.dtype)

    kv = dpq_ref[0] * (lane2 == t).astype(F32)
    for d in range(2):
        dt = jnp.exp(ls_ref[0, d])
        ar = ar_row_ref[0, d]
        ai = ai_row_ref[0, d]
        lam_row = ar * dt
        ang_row = ai * dt
        lam_col = ar_col_ref[0, d] * dt
        ang_col = ai_col_ref[0, d] * dt
        mag = jnp.exp(lam_row)
        ab_re = mag * jnp.cos(ang_row)
        ab_im = mag * jnp.sin(ang_row)
        den = ar * ar + ai * ai
        num_re = ab_re - 1.0
        f_re = (num_re * ar + ab_im * ai) / den
        f_im = (ab_im * ar - num_re * ai) / den
        bbt_re = f_re * bt_re_ref[0, d] - f_im * bt_im_ref[0, d]
        bbt_im = f_re * bt_im_ref[0, d] + f_im * bt_re_ref[0, d]

        if d == 0:
            lag2 = jnp.maximum(lane2 - t, 0).astype(F32)
            msk2 = (lane2 >= t).astype(F32)
        else:
            lag2 = jnp.maximum(t - lane2, 0).astype(F32)
            msk2 = (lane2 <= t).astype(F32)
        pm = jnp.exp(lam_col * lag2) * msk2
        pt_re = pm * jnp.cos(ang_col * lag2)
        pt_im = pm * jnp.sin(ang_col * lag2)
        cb_re = jnp.concatenate(
            [c_re_ref[0, d, pl.ds(p, 1), :] * bbt_re - c_im_ref[0, d, pl.ds(p, 1), :] * bbt_im
             for p in range(p_dim)], axis=0)
        cb_im = jnp.concatenate(
            [c_re_ref[0, d, pl.ds(p, 1), :] * bbt_im + c_im_ref[0, d, pl.ds(p, 1), :] * bbt_re
             for p in range(p_dim)], axis=0)
        kv = kv + (jnp.dot(cb_re, pt_re, preferred_element_type=F32, precision=HIGHEST)
                   - jnp.dot(cb_im, pt_im, preferred_element_type=F32, precision=HIGHEST))

        lag_col = (t - 1.0 - sub1) if d == 0 else sub1
        pw_m = jnp.exp(lag_col * lam_row)
        pw_re = pw_m * jnp.cos(lag_col * ang_row)
        pw_im = pw_m * jnp.sin(lag_col * ang_row)
        for q in range(p_dim):
            br = bbt_re[q:q + 1, :]
            bi = bbt_im[q:q + 1, :]
            bs_ref[0, q * t:(q + 1) * t, (2 * d) * LANES:(2 * d) * LANES + n] = \
                (pw_re * br - pw_im * bi).astype(bs_ref.dtype)
            bs_ref[0, q * t:(q + 1) * t, (2 * d + 1) * LANES:(2 * d + 1) * LANES + n] = \
                (pw_re * bi + pw_im * br).astype(bs_ref.dtype)

        lag_row = (lane1 + 1.0) if d == 0 else (t - lane1)
        qm = jnp.exp(lam_col * lag_row)
        q_re = qm * jnp.cos(ang_col * lag_row)
        q_im = qm * jnp.sin(ang_col * lag_row)
        for p in range(p_dim):
            cr = ct_re_ref[0, d, :, p:p + 1]
            ci = ct_im_ref[0, d, :, p:p + 1]
            wst_ref[0, (2 * d) * LANES:(2 * d) * LANES + n, p * t:(p + 1) * t] = \
                (cr * q_re - ci * q_im).astype(wst_ref.dtype)
            wst_ref[0, (2 * d + 1) * LANES:(2 * d + 1) * LANES + n, p * t:(p + 1) * t] = \
                (-(cr * q_im + ci * q_re)).astype(wst_ref.dtype)

        tt = float(t)
        at_m = jnp.exp(lam_row * tt)
        coef_ref[0, 2 * d:2 * d + 1, 0:n] = at_m * jnp.cos(ang_row * tt)
        coef_ref[0, 2 * d + 1:2 * d + 2, 0:n] = at_m * jnp.sin(ang_row * tt)

    kv_ref[...] = kv

    def toep_block(p, carry):
        kv_p = kv_ref[pl.ds(pl.multiple_of(p * p_dim, p_dim), p_dim), :]
        for q in range(p_dim):
            row = jnp.broadcast_to(kv_p[q:q + 1, :], (t, 2 * t))
            rolled = pltpu.roll(row, 0, axis=1, stride=1, stride_axis=0)
            toep_ref[0, q * t:(q + 1) * t, pl.ds(pl.multiple_of(p * t, t), t)] = \
                rolled[:, t:].astype(toep_ref.dtype)
        return carry

    lax.fori_loop(0, p_dim, toep_block, 0)


def _s5_prep_call(a_re, a_im, log_step, b_re, b_im, c_re, c_im, d_skip):
    g_dim = a_re.shape[1]
    n, p_dim, t = SSM_N, SSM_P, T_CHUNK
    gm = lambda v: jnp.swapaxes(v, 0, 1)
    ar, ai = gm(a_re), gm(a_im)
    dpq = (jnp.eye(p_dim, dtype=F32)[None] * d_skip.reshape(g_dim, p_dim, 1)).reshape(g_dim, p_dim * p_dim, 1)
    args = (ar[:, :, None, :], ai[:, :, None, :], ar[:, :, :, None], ai[:, :, :, None],
            gm(log_step)[:, :, None, None],
            jnp.swapaxes(gm(b_re), 2, 3), jnp.swapaxes(gm(b_im), 2, 3),
            gm(c_re), gm(c_im), jnp.swapaxes(gm(c_re), 2, 3), jnp.swapaxes(gm(c_im), 2, 3), dpq)

    def spec(v):
        blk = (1,) + v.shape[1:]
        nd = v.ndim
        return pl.BlockSpec(blk, lambda g: (g,) + (0,) * (nd - 1))

    out_shapes = [jax.ShapeDtypeStruct((g_dim, p_dim * t, p_dim * t), BF16),
                  jax.ShapeDtypeStruct((g_dim, p_dim * t, 4 * LANES), BF16),
                  jax.ShapeDtypeStruct((g_dim, 4 * LANES, p_dim * t), BF16),
                  jax.ShapeDtypeStruct((g_dim, SUBLANES, LANES), F32)]
    return pl.pallas_call(
        _s5_prep_body,
        grid=(g_dim,),
        in_specs=[spec(v) for v in args],
        out_specs=[spec(s) for s in out_shapes],
        out_shape=out_shapes,
        scratch_shapes=[pltpu.VMEM((p_dim * p_dim, 2 * t), F32)],
        compiler_params=_cparams(1),
        name="s5_prep",
    )(*args)


def _s5_body(*refs, nb, nc, with_y):
    if with_y:
        u_ref, bs_ref, coef_ref, h0_ref, toep_ref, wst_ref, y_ref, fin_ref, s_scr, h_scr = refs
    else:
        u_ref, bs_ref, coef_ref, h0_ref, fin_ref, s_scr = refs
    p_dim = u_ref.shape[1]
    t = T_CHUNK
    u = jnp.concatenate([u_ref[0, q] for q in range(p_dim)], axis=1).astype(BF16)
    s_scr[...] = jnp.dot(u, bs_ref[0], preferred_element_type=F32)
    if with_y:
        h_scr[...] = jnp.zeros(h_scr.shape, h_scr.dtype)

    af_re, af_im = coef_ref[0, 0:1, :], coef_ref[0, 1:2, :]
    ab_re, ab_im = coef_ref[0, 2:3, :], coef_ref[0, 3:4, :]
    init = tuple(h0_ref[0, r:r + 1, :] for r in range(4 * nb))

    blk = SUBLANES if nc % SUBLANES == 0 else nc

    def block(k, carry):
        new = list(carry)
        for b in range(nb):
            f0 = b * nc + blk * k
            b0 = b * nc + nc - blk - blk * k
            if blk == SUBLANES:
                f0 = pl.multiple_of(f0, SUBLANES)
                b0 = pl.multiple_of(b0, SUBLANES)
            sf_re = s_scr[pl.ds(f0, blk), 0:LANES]
            sf_im = s_scr[pl.ds(f0, blk), LANES:2 * LANES]
            sb_re = s_scr[pl.ds(b0, blk), 2 * LANES:3 * LANES]
            sb_im = s_scr[pl.ds(b0, blk), 3 * LANES:4 * LANES]
            fr, fi, br, bi = new[4 * b:4 * b + 4]
            hf_re, hf_im, hb_re, hb_im = [], [], [], []
            for j in range(blk):
                jb = blk - 1 - j
                hf_re.append(fr)
                hf_im.append(fi)
                hb_re.insert(0, br)
                hb_im.insert(0, bi)
                fr, fi = (af_re * fr - af_im * fi + sf_re[j:j + 1],
                          af_re * fi + af_im * fr + sf_im[j:j + 1])
                br, bi = (ab_re * br - ab_im * bi + sb_re[jb:jb + 1],
                          ab_re * bi + ab_im * br + sb_im[jb:jb + 1])
            if with_y:
                h_scr[pl.ds(f0, blk), 0:LANES] = jnp.concatenate(hf_re, axis=0)
                h_scr[pl.ds(f0, blk), LANES:2 * LANES] = jnp.concatenate(hf_im, axis=0)
                h_scr[pl.ds(b0, blk), 2 * LANES:3 * LANES] = jnp.concatenate(hb_re, axis=0)
                h_scr[pl.ds(b0, blk), 3 * LANES:4 * LANES] = jnp.concatenate(hb_im, axis=0)
            new[4 * b:4 * b + 4] = [fr, fi, br, bi]
        return tuple(new)

    if nc == blk:
        fin = block(0, init)
    else:
        fin = lax.fori_loop(0, nc // blk, block, init)
    fin_ref[0] = jnp.concatenate(list(fin) + [jnp.zeros((1, LANES), F32)] * (SUBLANES - 4 * nb), axis=0) \
        if 4 * nb < SUBLANES else jnp.concatenate(list(fin), axis=0)

    if with_y:
        y = jnp.dot(u, toep_ref[0], preferred_element_type=F32)
        y = y + jnp.dot(h_scr[...].astype(BF16), wst_ref[0], preferred_element_type=F32)
        y = _gelu(y)
        for p in range(p_dim):
            y_ref[0, p] = y[:, p * t:(p + 1) * t]


def _s5_call(u4, bs, coef, h0, toep=None, wst=None, *, nb, nc):
    g_dim, p_dim, rows, t = u4.shape
    with_y = toep is not None
    body = functools.partial(_s5_body, nb=nb, nc=nc, with_y=with_y)

    def spec(v):
        blk = (1,) + v.shape[1:]
        nd = v.ndim
        return pl.BlockSpec(blk, lambda g: (g,) + (0,) * (nd - 1))

    fin_shape = jax.ShapeDtypeStruct((g_dim, SUBLANES, LANES), F32)
    args = [u4, bs, coef, h0]
    scratch = [pltpu.VMEM((rows, 4 * LANES), F32)]
    if with_y:
        args += [toep, wst]
        out_shape = [jax.ShapeDtypeStruct(u4.shape, F32), fin_shape]
        scratch.append(pltpu.VMEM((rows, 4 * LANES), F32))
    else:
        out_shape = [fin_shape]
    outs = pl.pallas_call(
        body,
        grid=(g_dim,),
        in_specs=[spec(v) for v in args],
        out_specs=[spec(s) for s in out_shape],
        out_shape=out_shape,
        scratch_shapes=scratch,
        compiler_params=_cparams(1),
        name="s5_chunks" if with_y else "s5_ctx_states",
    )(*args)
    return outs


def _even_out_body(yc_ref, yt_ref, gwt_ref, gb_ref, woa_ref, wob_ref, x_ref, g1_ref, o_ref):
    yt = yt_ref[...]
    gl = jnp.dot(gwt_ref[...], yt.astype(BF16), preferred_element_type=F32) + gb_ref[...]
    ys = (yt * _sigmoid(gl)).astype(BF16)
    mix = jnp.dot(yc_ref[...], woa_ref[...], preferred_element_type=F32)
    mix = mix + jnp.dot(ys.T, wob_ref[...], preferred_element_type=F32)
    o_ref[...] = x_ref[...] + g1_ref[0] * mix


def _even_out_call(yc, yt, gwt, gb, w_out, x2d, g1, seq_len, tm):
    tokens, d = x2d.shape
    cw = yc.shape[1]
    sw = yt.shape[0]
    tiles_per_seq = seq_len // tm
    return pl.pallas_call(
        _even_out_body,
        grid=(tokens // tm,),
        in_specs=[pl.BlockSpec((tm, cw), lambda i: (i, 0)),
                  pl.BlockSpec((sw, tm), lambda i: (0, i)),
                  _resident((sw, sw)), _resident((sw, 1)),
                  _resident((cw, d), (0, 0)), _resident((sw, d), (cw // sw, 0)),
                  pl.BlockSpec((tm, d), lambda i: (i, 0)),
                  pl.BlockSpec((1, 1, d), lambda i: (i // tiles_per_seq, 0, 0))],
        out_specs=pl.BlockSpec((tm, d), lambda i: (i, 0)),
        out_shape=jax.ShapeDtypeStruct((tokens, d), F32),
        compiler_params=_cparams(1),
        name="even_out_proj",
    )(yc, yt, gwt, gb, w_out, w_out, x2d, g1)


def _odd_body(x_ref, g_ref, sh_ref, sc_ref, wu_ref, wv_ref, lg_ref, lb_ref, sw_ref, sb_ref,
              wo_ref, g1_ref, o_ref, gated_ref):
    x = x_ref[...]
    tm = x.shape[0]
    hn = _norm_mod(x, g_ref[...], sh_ref[0], sc_ref[0]).astype(BF16)
    v = _gelu(jnp.dot(hn, wv_ref[...], preferred_element_type=F32))
    v = _layer_norm(v, lg_ref[...], lb_ref[...]).astype(BF16)
    u = _gelu(jnp.dot(hn, wu_ref[...], preferred_element_type=F32))
    hd = v.shape[1] // SGU_HEADS
    for ch in range(tm // SGU_CHUNK):
        r0 = ch * SGU_CHUNK
        for h in range(SGU_HEADS):
            s = jnp.dot(sw_ref[h], v[r0:r0 + SGU_CHUNK, h * hd:(h + 1) * hd],
                        preferred_element_type=F32) + sb_ref[h]
            gated_ref[r0:r0 + SGU_CHUNK, h * hd:(h + 1) * hd] = \
                (u[r0:r0 + SGU_CHUNK, h * hd:(h + 1) * hd] * s).astype(BF16)
    mix = jnp.dot(gated_ref[...], wo_ref[...], preferred_element_type=F32)
    o_ref[...] = x + g1_ref[0] * mix


def _odd_call(x2d, g, shift, scale, w_in, ln_g, ln_b, sgu_w, sgu_b, wo, g1, seq_len, tm):
    tokens, d = x2d.shape
    w = w_in.shape[1] // 2
    tiles_per_seq = seq_len // tm
    mod_spec = pl.BlockSpec((1, 1, d), lambda i: (i // tiles_per_seq, 0, 0))
    return pl.pallas_call(
        _odd_body,
        grid=(tokens // tm,),
        in_specs=[pl.BlockSpec((tm, d), lambda i: (i, 0)),
                  _resident((1, d)), mod_spec, mod_spec,
                  _resident((d, w), (0, 0)), _resident((d, w), (0, 1)), _resident((1, w)), _resident((1, w)),
                  _resident(sgu_w.shape), _resident(sgu_b.shape),
                  _resident((w, d)), mod_spec],
        out_specs=pl.BlockSpec((tm, d), lambda i: (i, 0)),
        out_shape=jax.ShapeDtypeStruct((tokens, d), F32),
        scratch_shapes=[pltpu.VMEM((tm, w), BF16)],
        compiler_params=_cparams(1),
        name="odd_gmlp",
    )(x2d, g, shift, scale, w_in, w_in, ln_g, ln_b, sgu_w, sgu_b, wo, g1)


def _row_from_left(v):
    n_vr = GRID_W // SUBLANES
    first = lax.broadcasted_iota(jnp.int32, (SUBLANES, LANES), 0) == 0
    r = [pltpu.roll(v[SUBLANES * k:SUBLANES * (k + 1)], 1, axis=0) for k in range(n_vr)]
    return jnp.concatenate([jnp.where(first, r[k - 1] if k else 0.0, r[k]) for k in range(n_vr)], axis=0)


def _row_from_right(v):
    n_vr = GRID_W // SUBLANES
    last = lax.broadcasted_iota(jnp.int32, (SUBLANES, LANES), 0) == SUBLANES - 1
    r = [pltpu.roll(v[SUBLANES * k:SUBLANES * (k + 1)], SUBLANES - 1, axis=0) for k in range(n_vr)]
    return jnp.concatenate([jnp.where(last, r[k + 1] if k < n_vr - 1 else 0.0, r[k])
                            for k in range(n_vr)], axis=0)


def _conv3x3_row(up, mid, dn, cw_ref, cb_ref, c0):
    w = lambda k: cw_ref[pl.ds(k, 1), pl.ds(c0, LANES)]
    left = up * w(0) + mid * w(3) + dn * w(6)
    cent = up * w(1) + mid * w(4) + dn * w(7)
    right = up * w(2) + mid * w(5) + dn * w(8)
    return cent + _row_from_left(left) + _row_from_right(right) + cb_ref[:, pl.ds(c0, LANES)]


def _ffn_up_body(x_ref, g_ref, sh_ref, sc_ref, wa_ref, wg_ref, cw_ref, cb_ref, act_ref, zb_ref,
                 hn_ref, z_ref):
    @pl.when(pl.program_id(1) == 0)
    def _():
        hn_ref[...] = _norm_mod(x_ref[...], g_ref[...], sh_ref[0], sc_ref[0]).astype(BF16)

    tm, tn = z_ref.shape
    half = tn // 2
    n_rows = tm // GRID_W
    n_cb = half // LANES
    z_ref[:, 0:half] = jnp.dot(hn_ref[...], wa_ref[...], preferred_element_type=F32)
    z_ref[:, half:tn] = jnp.dot(hn_ref[...], wg_ref[...], preferred_element_type=F32)

    for e, r in enumerate((0, 1, n_rows - 2, n_rows - 1)):
        zb_ref[0, e] = z_ref[r * GRID_W:(r + 1) * GRID_W, :].astype(zb_ref.dtype)
    act_ref[0:GRID_W, :] = jnp.zeros((GRID_W, half), act_ref.dtype)
    act_ref[tm - GRID_W:tm, :] = jnp.zeros((GRID_W, half), act_ref.dtype)

    def item(r, carry):
        r0 = pl.multiple_of((1 + r) * GRID_W, GRID_W)

        def conv(c0):
            rows = [z_ref[pl.ds(r0 + dy * GRID_W, GRID_W), c0:c0 + LANES] for dy in (-1, 0, 1)]
            return _conv3x3_row(*rows, cw_ref, cb_ref, c0)

        for cb in range(n_cb):
            a = conv(cb * LANES)
            g = conv(half + cb * LANES)
            act_ref[pl.ds(r0, GRID_W), cb * LANES:(cb + 1) * LANES] = (a * _silu(g)).astype(act_ref.dtype)
        return carry

    lax.fori_loop(0, n_rows - 2, item, 0)


def _ffn_up_call(x2d, g, shift, scale, w, cw, cb, seq_len, tm, tn):
    tokens, d = x2d.shape
    n2 = w.shape[1]
    half = tn // 2
    gate_blk0 = (n2 // 2) // half
    tiles_per_seq = seq_len // tm
    mod_spec = pl.BlockSpec((1, 1, d), lambda i, j: (i // tiles_per_seq, 0, 0))
    return pl.pallas_call(
        _ffn_up_body,
        grid=(tokens // tm, n2 // tn),
        in_specs=[pl.BlockSpec((tm, d), lambda i, j: (i, 0)),
                  _resident((1, d)), mod_spec, mod_spec,
                  pl.BlockSpec((d, half), lambda i, j: (0, j)),
                  pl.BlockSpec((d, half), lambda i, j: (0, gate_blk0 + j)),
                  pl.BlockSpec((FFN_TAP_ROWS, tn), lambda i, j: (0, j)),
                  pl.BlockSpec((1, tn), lambda i, j: (0, j))],
        out_specs=[pl.BlockSpec((tm, half), lambda i, j: (i, j)),
                   pl.BlockSpec((1, 4, GRID_W, tn), lambda i, j: (i, 0, 0, j))],
        out_shape=[jax.ShapeDtypeStruct((tokens, n2 // 2), BF16),
                   jax.ShapeDtypeStruct((tokens // tm, 4, GRID_W, n2), BF16)],
        scratch_shapes=[pltpu.VMEM((tm, d), BF16), pltpu.VMEM((tm, tn), F32)],
        compiler_params=_cparams(2),
        name="ffn_up_conv",
    )(x2d, g, shift, scale, w, w, cw, cb)


def _ffn_down_body(act_ref, zc_ref, zp_ref, zn_ref, cw_ref, cb_ref, wo_ref, x_ref, g2_ref, fg_ref, o_ref,
                   acts_ref, *, td, tn, down_per_up, up_per_seq, final_norm):
    k = pl.program_id(0)
    up_tile = k // down_per_up
    pos = k % down_per_up
    top_ok = jnp.where((up_tile % up_per_seq) == 0, 0.0, 1.0)
    bot_ok = jnp.where((up_tile % up_per_seq) == up_per_seq - 1, 0.0, 1.0)
    dff = acts_ref.shape[1]
    half = tn // 2
    acts_ref[...] = act_ref[...]

    def edge_row(rows, dst):
        def blk(j, carry):
            for s in range(half // LANES):
                cz = pl.multiple_of(j * tn + s * LANES, LANES)
                cg = pl.multiple_of(cz + half, LANES)
                a = _conv3x3_row(*rows(cz), cw_ref, cb_ref, cz)
                g = _conv3x3_row(*rows(cg), cw_ref, cb_ref, cg)
                acts_ref[dst:dst + GRID_W, pl.ds(pl.multiple_of(j * half + s * LANES, LANES), LANES)] = \
                    (a * _silu(g)).astype(acts_ref.dtype)
            return carry

        lax.fori_loop(0, 2 * dff // tn, blk, 0)

    @pl.when(pos == 0)
    def _():
        edge_row(lambda c: (zp_ref[0, 0, :, pl.ds(c, LANES)].astype(F32) * top_ok,
                            zc_ref[0, 0, :, pl.ds(c, LANES)].astype(F32),
                            zc_ref[0, 1, :, pl.ds(c, LANES)].astype(F32)), 0)

    @pl.when(pos == down_per_up - 1)
    def _():
        edge_row(lambda c: (zc_ref[0, 2, :, pl.ds(c, LANES)].astype(F32),
                            zc_ref[0, 3, :, pl.ds(c, LANES)].astype(F32),
                            zn_ref[0, 0, :, pl.ds(c, LANES)].astype(F32) * bot_ok), td - GRID_W)

    y = x_ref[...] + g2_ref[0] * jnp.dot(acts_ref[...], wo_ref[...], preferred_element_type=F32)
    if final_norm:
        ms = jnp.mean(y * y, axis=-1, keepdims=True)
        y = y * lax.rsqrt(ms + RMS_EPS) * fg_ref[...]
    o_ref[...] = y


def _ffn_down_call(act, zb, cw, cb, wo, x2d, g2, final_g, seq_len, tm, tn, td, final_norm):
    tokens, d = x2d.shape
    dff = act.shape[1]
    n2 = zb.shape[3]
    n_up = zb.shape[0]
    down_per_up = tm // td
    tiles_per_seq = seq_len // td
    body = functools.partial(_ffn_down_body, td=td, tn=tn, down_per_up=down_per_up,
                             up_per_seq=seq_len // tm, final_norm=final_norm)
    return pl.pallas_call(
        body,
        grid=(tokens // td,),
        in_specs=[pl.BlockSpec((td, dff), lambda k: (k, 0)),
                  pl.BlockSpec((1, 4, GRID_W, n2), lambda k: (k // down_per_up, 0, 0, 0)),
                  pl.BlockSpec((1, 1, GRID_W, n2),
                               lambda k: (jnp.maximum(k // down_per_up - 1, 0), 3, 0, 0)),
                  pl.BlockSpec((1, 1, GRID_W, n2),
                               lambda k: (jnp.minimum(k // down_per_up + 1, n_up - 1), 0, 0, 0)),
                  _resident((FFN_TAP_ROWS, n2)), _resident((1, n2)), _resident((dff, d)),
                  pl.BlockSpec((td, d), lambda k: (k, 0)),
                  pl.BlockSpec((1, 1, d), lambda k: (k // tiles_per_seq, 0, 0)),
                  _resident((1, d))],
        out_specs=pl.BlockSpec((td, d), lambda k: (k, 0)),
        out_shape=jax.ShapeDtypeStruct((tokens, d), F32),
        scratch_shapes=[pltpu.VMEM((td, dff), BF16)],
        compiler_params=_cparams(1),
        name="ffn_edge_down",
    )(act, zb, zb, zb, cw, cb, wo, x2d, g2, final_g.reshape(1, d))


def _pick_tile(seq_len, want):
    tm = min(want, seq_len)
    assert seq_len % tm == 0
    return tm


def _tiles(seq_len):
    ffn_up = _pick_tile(seq_len, 1024)
    return dict(even_in=_pick_tile(seq_len, 512), conv=_pick_tile(seq_len, 512),
                even_out=_pick_tile(seq_len, 512), odd=_pick_tile(seq_len, 256),
                ffn_up=ffn_up, ffn_cols=1024, ffn_down=_pick_tile(ffn_up, 256))


def kernel(x, c, ctx, c_ctx, ada_w, ada_b, norm_g, final_g, e_w_in, e_conv_w, e_conv_b, e_ln_g, e_ln_b, s5_a_re, s5_a_im, s5_log_step, s5_b_re, s5_b_im, s5_c_re, s5_c_im, s5_d, s5_glu_w, s5_glu_b, e_w_out, o_w_in, o_ln_g, o_ln_b, o_sgu_w, o_sgu_b, o_w_out, f_w_in, f_conv_w, f_conv_b, f_w_out):
    bn, seq_len, d = x.shape
    ctx_len = ctx.shape[1]
    depth = ada_w.shape[0]
    assert depth == 2 and bn == 2, "block structure is specialised to depth 2, batch 2"
    tokens = bn * seq_len
    cw = e_conv_w.shape[2]
    sw = s5_d.shape[1]
    g_dim = sw // SSM_P
    t = T_CHUNK
    nc = seq_len // t
    nc_ctx = ctx_len // t
    assert seq_len % t == 0 and ctx_len % t == 0 and seq_len % GRID_W == 0
    assert cw % sw == 0, "out-proj row blocks are addressed in units of the S5 width"

    x2d = x.reshape(tokens, d)

    s_rows = jnp.zeros((SUBLANES, d), F32).at[:bn].set(c).at[bn].set(c_ctx)
    mods = _ada_call(s_rows, ada_w, ada_b)

    def lat_mod(layer, k):
        return mods[layer, :bn, k * d:(k + 1) * d].reshape(bn, 1, d)

    li = 0
    w_in = e_w_in[li]
    w_in_bf = w_in.astype(BF16)
    wut = w_in[:, 2 * cw:].T.astype(BF16)
    toep, bs, wst, coef = _s5_prep_call(s5_a_re[li], s5_a_im[li], s5_log_step[li], s5_b_re[li],
                                        s5_b_im[li], s5_c_re[li], s5_c_im[li], s5_d[li])

    g_row = norm_g[0, 0].reshape(1, d)
    ctx_shift = mods[0, bn:bn + 1, 0:d]
    ctx_scale = mods[0, bn:bn + 1, d:2 * d]
    uct = _ctx_in_call(ctx.reshape(bn * ctx_len, d), g_row, ctx_shift, ctx_scale, wut)
    uc4 = uct.reshape(g_dim, SSM_P, bn * nc_ctx, t)
    rows_ctx = -(-(bn * nc_ctx) // SUBLANES) * SUBLANES
    uc4 = jnp.pad(uc4, ((0, 0), (0, 0), (0, rows_ctx - bn * nc_ctx), (0, 0)))
    zero_h0 = jnp.zeros((g_dim, SUBLANES, LANES), F32)
    (h0s,) = _s5_call(uc4, bs, coef, zero_h0, nb=bn, nc=nc_ctx)

    tiles = _tiles(seq_len)
    z, ut = _even_in_call(x2d, g_row, lat_mod(0, 0), lat_mod(0, 1), w_in_bf, wut, cw, seq_len, tiles['even_in'])
    yc = _conv_call(z, e_conv_w[li], e_conv_b[li], e_ln_g[li], e_ln_b[li], seq_len,
                    tiles['conv'])
    u4 = ut.reshape(g_dim, SSM_P, bn * nc, t)
    y4, _ = _s5_call(u4, bs, coef, h0s, toep, wst, nb=bn, nc=nc)
    yt = y4.reshape(sw, tokens)
    w_out = e_w_out[li]
    h = _even_out_call(yc, yt, s5_glu_w[li].T.astype(BF16), s5_glu_b[li].reshape(sw, 1),
                       w_out.astype(BF16), x2d, lat_mod(0, 2),
                       seq_len, tiles['even_out'])

    def conv_ffn(h, layer, final_norm):
        tm, tn, td = tiles['ffn_up'], tiles['ffn_cols'], tiles['ffn_down']
        n2 = f_w_in.shape[2]

        def regroup(v):
            lead = v.shape[:-1]
            v = v.reshape(lead + (2, n2 // tn, tn // 2))
            return jnp.swapaxes(v, -3, -2).reshape(lead + (n2,))

        cw = jnp.pad(regroup(f_conv_w[layer].reshape(9, n2)), ((0, FFN_TAP_ROWS - 9), (0, 0)))
        cb = regroup(f_conv_b[layer]).reshape(1, n2)
        act, zb = _ffn_up_call(h, norm_g[layer, 1].reshape(1, d), lat_mod(layer, 3), lat_mod(layer, 4),
                               f_w_in[layer].astype(BF16), cw, cb, seq_len, tm, tn)
        return _ffn_down_call(act, zb, cw, cb, f_w_out[layer].astype(BF16), h, lat_mod(layer, 5),
                              final_g, seq_len, tm, tn, td, final_norm)

    h = conv_ffn(h, 0, False)

    li = 0
    w = o_w_in.shape[2] // 2
    h = _odd_call(h, norm_g[1, 0].reshape(1, d), lat_mod(1, 0), lat_mod(1, 1),
                  o_w_in[li].astype(BF16),
                  o_ln_g[li].reshape(1, w), o_ln_b[li].reshape(1, w),
                  o_sgu_w[li].astype(BF16), o_sgu_b[li].reshape(SGU_HEADS, SGU_CHUNK, 1),
                  o_w_out[li].astype(BF16), lat_mod(1, 2), seq_len, tiles['odd'])
    h = conv_ffn(h, 1, True)
    return h.reshape(bn, seq_len, d)
```
